```python
import functools
import numpy as np
import jax, jax.numpy as jnp
from jax import lax

D_MODEL = 2048
BATCH = 8
SEQ = 4096
DEPTH = 2

CTX_LEN = 256
GRID_W = 64
CHUNK = 64
BRANCH_W = D_MODEL // 2
N_BRANCH = 3
MLSTM_HEADS = 4
MLSTM_DH = BRANCH_W // MLSTM_HEADS
HGRN_HEADS = 8
HGRN_DH = BRANCH_W // HGRN_HEADS
RET_HEADS = 4
RET_DV = BRANCH_W // RET_HEADS
RET_DK = RET_DV // 2
RET_QK_W = RET_HEADS * RET_DK
CONV_K = 3
FFN_DIM = ((8 * D_MODEL // 3 + 255) // 256) * 256
N_EXPERTS = 8
TOP_K = 2
EXPERT_DIM = 7 * D_MODEL // 2
MOE_BLOCK = 256
N_DENSE = (DEPTH + 1) // 2
N_MOE = DEPTH // 2
ALPHA = (2 * DEPTH) ** 0.25
BETA = (8 * DEPTH) ** -0.25
EPS = 1e-6
NEG = -1e30
TINY = 1e-30
IN_SIZES = (2 * BRANCH_W, BRANCH_W, BRANCH_W, 4 * MLSTM_HEADS,
            BRANCH_W, BRANCH_W, BRANCH_W, BRANCH_W, BRANCH_W,
            RET_QK_W, RET_QK_W, BRANCH_W, BRANCH_W,
            N_BRANCH * D_MODEL)
N_IN = sum(IN_SIZES)

kernel_name = 'hybrid_bidir_mlstm_hgrn2_retention_moe'

f32 = jnp.float32


def layer_norm(x):
    xf = x.astype(f32)
    mu = xf.mean(-1, keepdims=True)
    var = jnp.mean(jnp.square(xf - mu), -1, keepdims=True)
    return ((xf - mu) * lax.rsqrt(var + EPS)).astype(x.dtype)


def modulate(x, shift, scale):
    return layer_norm(x) * (1 + scale) + shift


def post_norm(z, g, b):
    return layer_norm(z) * g + b


def heads(a, n):
    b, t, w = a.shape
    return a.reshape(b, t, n, w // n).transpose(0, 2, 1, 3)


def merge_heads_normed(h, rms):
    hf = h.astype(f32)
    if rms:
        hf = hf * lax.rsqrt(jnp.mean(hf * hf, -1, keepdims=True) + EPS)
    else:
        hf = layer_norm(hf)
    b, n, t, d = h.shape
    return hf.transpose(0, 2, 1, 3).reshape(b, t, n * d).astype(h.dtype)


def to_chunks(a):
    b, n, t = a.shape[:3]
    return jnp.moveaxis(a.reshape(b, n, t // CHUNK, CHUNK, *a.shape[3:]), 2, 0)


def from_chunks(o):
    nc, b, n, c = o.shape[:4]
    return jnp.moveaxis(o, 0, 2).reshape(b, n, nc * c, *o.shape[4:])


def causal_mask():
    return jnp.tril(jnp.ones((CHUNK, CHUNK), bool))


def mlstm_scan(state, q, k, v, log_i, log_f):
    dtype = v.dtype
    causal = causal_mask()
    xs = tuple(to_chunks(a.astype(f32)) for a in (q, k, v, log_i, log_f))

    def step(carry, inp):
        C, n, m = carry
        qc, kc, vc, li, lf = inp
        b = jnp.cumsum(lf, axis=-1)
        a_inter = b + m[..., None]
        d_intra = jnp.where(causal, b[..., :, None] - b[..., None, :] + li[..., None, :], NEG)
        m_t = jnp.maximum(a_inter, d_intra.max(-1))
        w_inter = jnp.exp(a_inter - m_t)
        s = jnp.einsum('bhtk,bhsk->bhts', qc, kc) * jnp.exp(d_intra - m_t[..., None])
        num = w_inter[..., None] * jnp.einsum('bhtk,bhkv->bhtv', qc, C) + jnp.einsum('bhts,bhsv->bhtv', s, vc)
        den = w_inter * jnp.einsum('bhtk,bhk->bht', qc, n) + s.sum(-1)
        h = num / jnp.maximum(jnp.abs(den), jnp.exp(-m_t))[..., None]
        g = b[..., -1:] - b + li
        m_new = jnp.maximum(b[..., -1] + m, g.max(-1))
        decay = jnp.exp(b[..., -1] + m - m_new)
        w_state = jnp.exp(g - m_new[..., None])
        C_new = decay[..., None, None] * C + jnp.einsum('bhs,bhsk,bhsv->bhkv', w_state, kc, vc)
        n_new = decay[..., None] * n + jnp.einsum('bhs,bhsk->bhk', w_state, kc)
        return (C_new, n_new, m_new), h

    state, hs = lax.scan(step, state, xs)
    return from_chunks(hs).astype(dtype), state


def gla_scan(state, q, k, v, log_f):
    dtype = v.dtype
    causal = causal_mask()
    xs = tuple(to_chunks(a.astype(f32)) for a in (q, k, v, log_f))

    def step(S, inp):
        qc, kc, vc, lf = inp
        b = jnp.cumsum(lf, axis=-2)
        rel = jnp.where(causal[:, :, None], b[:, :, :, None, :] - b[:, :, None, :, :], NEG)
        a = jnp.einsum('bhtk,bhsk,bhtsk->bhts', qc, kc, jnp.exp(rel))
        o = jnp.einsum('bhtk,bhkv->bhtv', qc * jnp.exp(b), S) + jnp.einsum('bhts,bhsv->bhtv', a, vc)
        b_last = b[:, :, -1:, :]
        S = jnp.exp(b_last)[:, :, 0, :, None] * S + jnp.einsum('bhsk,bhsv->bhkv', kc * jnp.exp(b_last - b), vc)
        return S, o

    state, os_ = lax.scan(step, state, xs)
    return from_chunks(os_).astype(dtype), state


def retention_scan(state, q, k, v, log_gamma):
    dtype = v.dtype
    causal = causal_mask().astype(f32)
    pos = jnp.arange(1, CHUNK + 1, dtype=f32)
    lg = log_gamma.astype(f32)[:, None]
    inter_w = jnp.exp(lg * pos)[:, :, None]
    state_w = jnp.exp(lg * (CHUNK - pos))[:, :, None]
    chunk_decay = jnp.exp(lg * CHUNK)[:, :, None]
    rel = jnp.where(causal > 0, pos[:, None] - pos[None, :], 0.0)
    dmat = jnp.exp(lg[:, :, None] * rel) * causal
    xs = tuple(to_chunks(a.astype(f32)) for a in (q, k, v))

    def step(S, inp):
        qc, kc, vc = inp
        o = inter_w * jnp.einsum('bhtk,bhkv->bhtv', qc, S) + jnp.einsum(
            'bhts,bhsv->bhtv', jnp.einsum('bhtk,bhsk->bhts', qc, kc) * dmat, vc)
        S = chunk_decay * S + jnp.einsum('bhsk,bhsv->bhkv', kc * state_w, vc)
        return S, o

    state, os_ = lax.scan(step, state, xs)
    return from_chunks(os_).astype(dtype), state


def bidirectional(scan_fw, scan_bw, ctx_fw, ctx_bw, lat_fw, lat_bw, state0, with_ctx):
    flip = lambda args: tuple(jnp.flip(a, axis=2) for a in args)
    oc_fw, sc_fw = scan_fw(state0, *ctx_fw)
    oc_bw, sc_bw = scan_bw(state0, *flip(ctx_bw))
    ol_fw, _ = scan_fw(sc_fw, *lat_fw)
    ol_bw, _ = scan_bw(sc_bw, *flip(lat_bw))
    lat = ol_fw + jnp.flip(ol_bw, axis=2)
    ctx_out = oc_fw + jnp.flip(oc_bw, axis=2) if with_ctx else None
    return lat, ctx_out


def short_conv(a, w, b, grid):
    ch = a.shape[-1]
    if grid:
        bsz, t, _ = a.shape
        rows = t // GRID_W
        out = lax.conv_general_dilated(a.reshape(bsz, rows, GRID_W, ch), w[:, :, None, :], (1, 1), 'SAME',
                                       dimension_numbers=('NHWC', 'HWIO', 'NHWC'),
                                       feature_group_count=ch).reshape(bsz, t, ch)
    else:
        out = lax.conv_general_dilated(a, w[CONV_K // 2][:, None, :], (1,), 'SAME',
                                       dimension_numbers=('NWC', 'WIO', 'NWC'), feature_group_count=ch)
    return out + b


def mixer_inputs(u, w_in_l, conv_w_l, conv_b_l, mgate_b_l, lb_l, grid):
    bsz, t, _ = u.shape
    split_idx = np.cumsum(IN_SIZES)[:-1].tolist()
    (mqk, mv, mz, mgate, hq, hi, hf_fw, hf_bw, hg, rq, rk, rv, rg, merge) = jnp.split(u @ w_in_l, split_idx, axis=-1)
    mq, mk = jnp.split(jax.nn.silu(short_conv(mqk, conv_w_l, conv_b_l, grid)), 2, axis=-1)
    gates = (mgate + mgate_b_l).astype(f32).reshape(bsz, t, 4, MLSTM_HEADS).transpose(2, 0, 3, 1)
    mq_h = heads(mq, MLSTM_HEADS)
    mk_h = heads(mk, MLSTM_HEADS) * MLSTM_DH ** -0.5
    mv_h = heads(mv, MLSTM_HEADS)
    mlstm_fw = (mq_h, mk_h, mv_h, gates[0], jax.nn.log_sigmoid(gates[1]))
    mlstm_bw = (mq_h, mk_h, mv_h, gates[2], jax.nn.log_sigmoid(gates[3]))
    hq_h = heads(jax.nn.silu(hq), HGRN_HEADS)
    hv_h = heads(hi, HGRN_HEADS)
    hgrn = []
    for d, hf in enumerate((hf_fw, hf_bw)):
        lb = lb_l[d].reshape(1, HGRN_HEADS, 1, HGRN_DH)
        ft = heads(hf, HGRN_HEADS).astype(f32)
        one_minus_f = (1.0 - lb) * jax.nn.sigmoid(-ft)
        log_f = jnp.log(jnp.maximum(lb + (1.0 - lb) * jax.nn.sigmoid(ft), TINY))
        hgrn.append((hq_h, one_minus_f, hv_h, log_f))
    ret = (heads(rq, RET_HEADS), heads(rk, RET_HEADS) * RET_DK ** -0.5, heads(rv, RET_HEADS))
    return {'mlstm_fw': mlstm_fw, 'mlstm_bw': mlstm_bw, 'hgrn_fw': hgrn[0], 'hgrn_bw': hgrn[1],
            'ret': ret, 'post': (mz, hg, rg, merge)}


def merge_branches(ym, yh, yr, post, norm_g, w_br, w_o):
    mz, hg, rg, merge = post
    bsz, t, _ = mz.shape
    branches = (merge_heads_normed(ym, False) * norm_g[0] * jax.nn.silu(mz),
                merge_heads_normed(yh, True) * norm_g[1] * jax.nn.silu(hg),
                merge_heads_normed(yr, False) * norm_g[2] * jax.nn.silu(rg))
    gates = jax.nn.sigmoid(merge.reshape(bsz, t, N_BRANCH, D_MODEL))
    mixed = gates[:, :, 0] * (branches[0] @ w_br[0])
    for n in range(1, N_BRANCH):
        mixed = mixed + gates[:, :, n] * (branches[n] @ w_br[n])
    return mixed @ w_o


def swiglu(u, w_gate, w_up, w_down):
    return (jax.nn.silu(u @ w_gate) * (u @ w_up)) @ w_down


def moe_swiglu(u, w_router, w_gate, w_up, w_down):
    shp = u.shape
    xt = u.reshape(-1, D_MODEL)
    t = xt.shape[0]
    logits = (xt @ w_router).astype(f32)
    top_v, top_e = lax.top_k(logits, TOP_K)
    weights = jax.nn.softmax(top_v, axis=-1)
    e_flat = top_e.reshape(-1)
    tok_flat = jnp.repeat(jnp.arange(t), TOP_K)
    w_flat = weights.reshape(-1)
    n_assign = t * TOP_K
    order = jnp.argsort(e_flat)
    e_sorted, tok_sorted, w_sorted = e_flat[order], tok_flat[order], w_flat[order]
    sizes = jnp.bincount(e_flat, length=N_EXPERTS)
    padded = ((sizes + MOE_BLOCK - 1) // MOE_BLOCK) * MOE_BLOCK
    pad_end = jnp.cumsum(padded)
    pad_start = pad_end - padded
    grp_start = jnp.cumsum(sizes) - sizes
    dest = pad_start[e_sorted] + jnp.arange(n_assign) - grp_start[e_sorted]
    n_blocks = -(-n_assign // MOE_BLOCK) + N_EXPERTS
    buf = jnp.zeros((n_blocks * MOE_BLOCK, D_MODEL), u.dtype).at[dest].set(xt[tok_sorted])
    block_e = jnp.minimum(jnp.searchsorted(pad_end, jnp.arange(n_blocks) * MOE_BLOCK, side='right'), N_EXPERTS - 1)

    def expert_block(args):
        xb, e = args
        return swiglu(xb, w_gate[e], w_up[e], w_down[e])

    yb = lax.map(expert_block, (buf.reshape(n_blocks, MOE_BLOCK, D_MODEL), block_e))
    y = yb.reshape(-1, D_MODEL)[dest] * w_sorted[:, None].astype(u.dtype)
    out = jnp.zeros_like(xt).at[tok_sorted].add(y)
    return out.reshape(shp)


def setup_inputs(seed: int = 0) -> dict:
    key = jax.random.key(seed)
    ks = jax.random.split(key, 24)
    nrm = lambda k, shape, s: jax.random.normal(k, shape, f32) * s
    x = nrm(ks[0], (BATCH, SEQ, D_MODEL), 1.0)
    c = nrm(ks[1], (BATCH, D_MODEL), 1.0)
    ctx = nrm(ks[2], (BATCH, CTX_LEN, D_MODEL), 1.0)
    c_ctx = nrm(ks[3], (D_MODEL,), 1.0)
    w_ada = nrm(ks[4], (DEPTH, D_MODEL, 6 * D_MODEL), 0.5 * D_MODEL ** -0.5)
    b_ada = nrm(ks[5], (DEPTH, 6 * D_MODEL), 0.01)
    w_in = nrm(ks[6], (DEPTH, D_MODEL, N_IN), D_MODEL ** -0.5)
    conv_w = nrm(ks[7], (DEPTH, CONV_K, CONV_K, 2 * BRANCH_W), 1.0 / CONV_K)
    conv_b = nrm(ks[8], (DEPTH, 2 * BRANCH_W), 0.01)
    kig, kfg = jax.random.split(ks[9])
    ig = nrm(kig, (DEPTH, 2, MLSTM_HEADS), 0.1)
    fg = jnp.linspace(3.0, 6.0, MLSTM_HEADS) + nrm(kfg, (DEPTH, 2, MLSTM_HEADS), 0.1)
    mlstm_gate_b = jnp.stack([ig[:, 0], fg[:, 0], ig[:, 1], fg[:, 1]], axis=1).reshape(DEPTH, 4 * MLSTM_HEADS)
    hgrn_lb = nrm(ks[10], (2, DEPTH, BRANCH_W), 0.1)
    ret_decay_logit = jnp.log(2.0 ** (5.0 + jnp.arange(RET_HEADS, dtype=f32)) - 1.0) + nrm(ks[11], (DEPTH, 2, RET_HEADS), 0.1)
    head_norm_g = 1.0 + nrm(ks[12], (DEPTH, N_BRANCH, BRANCH_W), 0.02)
    w_branch = nrm(ks[13], (DEPTH, N_BRANCH, BRANCH_W, D_MODEL), BETA * BRANCH_W ** -0.5)
    w_out = nrm(ks[14], (DEPTH, D_MODEL, D_MODEL), BETA * D_MODEL ** -0.5)
    post_ln_g = 1.0 + nrm(ks[15], (DEPTH, 2, D_MODEL), 0.02)
    post_ln_b = nrm(ks[16], (DEPTH, 2, D_MODEL), 0.02)
    ffn_w_gate = nrm(ks[17], (N_DENSE, D_MODEL, FFN_DIM), D_MODEL ** -0.5)
    ffn_w_up = nrm(ks[18], (N_DENSE, D_MODEL, FFN_DIM), BETA * D_MODEL ** -0.5)
    ffn_w_down = nrm(ks[19], (N_DENSE, FFN_DIM, D_MODEL), BETA * FFN_DIM ** -0.5)
    moe_w_router = nrm(ks[20], (N_MOE, D_MODEL, N_EXPERTS), D_MODEL ** -0.5)
    moe_w_gate = nrm(ks[21], (N_MOE, N_EXPERTS, D_MODEL, EXPERT_DIM), D_MODEL ** -0.5)
    moe_w_up = nrm(ks[22], (N_MOE, N_EXPERTS, D_MODEL, EXPERT_DIM), BETA * D_MODEL ** -0.5)
    moe_w_down = nrm(ks[23], (N_MOE, N_EXPERTS, EXPERT_DIM, D_MODEL), BETA * EXPERT_DIM ** -0.5)
    return {'x': x, 'c': c, 'ctx': ctx, 'c_ctx': c_ctx, 'w_ada': w_ada, 'b_ada': b_ada,
            'w_in': w_in, 'conv_w': conv_w, 'conv_b': conv_b, 'mlstm_gate_b': mlstm_gate_b,
            'hgrn_lb': hgrn_lb, 'ret_decay_logit': ret_decay_logit, 'head_norm_g': head_norm_g,
            'w_branch': w_branch, 'w_out': w_out, 'post_ln_g': post_ln_g, 'post_ln_b': post_ln_b,
            'ffn_w_gate': ffn_w_gate, 'ffn_w_up': ffn_w_up, 'ffn_w_down': ffn_w_down,
            'moe_w_router': moe_w_router, 'moe_w_gate': moe_w_gate, 'moe_w_up': moe_w_up,
            'moe_w_down': moe_w_down}


def reference(x, c, ctx, c_ctx, w_ada, b_ada, w_in, conv_w, conv_b, mlstm_gate_b, hgrn_lb,
              ret_decay_logit, head_norm_g, w_branch, w_out, post_ln_g, post_ln_b,
              ffn_w_gate, ffn_w_up, ffn_w_down, moe_w_router, moe_w_gate, moe_w_up, moe_w_down):
    bsz = x.shape[0]
    lbs = jax.nn.softmax(hgrn_lb.astype(f32), axis=1)
    lbs = jnp.cumsum(lbs, axis=1) - lbs[:, :1]
    m_state0 = (jnp.zeros((bsz, MLSTM_HEADS, MLSTM_DH, MLSTM_DH), f32),
                jnp.zeros((bsz, MLSTM_HEADS, MLSTM_DH), f32),
                jnp.full((bsz, MLSTM_HEADS), NEG, f32))
    h_state0 = jnp.zeros((bsz, HGRN_HEADS, HGRN_DH, HGRN_DH), f32)
    r_state0 = jnp.zeros((bsz, RET_HEADS, RET_DK, RET_DV), f32)
    h = ctx
    for l in range(DEPTH):
        need_ctx = l < DEPTH - 1
        mod_x = jnp.split((jax.nn.silu(c) @ w_ada[l] + b_ada[l])[:, None, :], 6, axis=-1)
        mod_h = jnp.split(jax.nn.silu(c_ctx) @ w_ada[l] + b_ada[l], 6, axis=-1)
        ix = mixer_inputs(modulate(x, mod_x[0], mod_x[1]), w_in[l], conv_w[l], conv_b[l],
                          mlstm_gate_b[l], lbs[:, l], True)
        ih = mixer_inputs(modulate(h, mod_h[0], mod_h[1]), w_in[l], conv_w[l], conv_b[l],
                          mlstm_gate_b[l], lbs[:, l], False)
        ym_x, ym_h = bidirectional(mlstm_scan, mlstm_scan, ih['mlstm_fw'], ih['mlstm_bw'],
                                   ix['mlstm_fw'], ix['mlstm_bw'], m_state0, need_ctx)
        yh_x, yh_h = bidirectional(gla_scan, gla_scan, ih['hgrn_fw'], ih['hgrn_bw'],
                                   ix['hgrn_fw'], ix['hgrn_bw'], h_state0, need_ctx)
        log_gamma = jax.nn.log_sigmoid(ret_decay_logit[l].astype(f32))
        yr_x, yr_h = bidirectional(functools.partial(retention_scan, log_gamma=log_gamma[0]),
                                   functools.partial(retention_scan, log_gamma=log_gamma[1]),
                                   ih['ret'], ih['ret'], ix['ret'], ix['ret'], r_state0, need_ctx)
        mix_x = merge_branches(ym_x, yh_x, yr_x, ix['post'], head_norm_g[l], w_branch[l], w_out[l])
        x = post_norm(ALPHA * x + mod_x[2] * mix_x, post_ln_g[l, 0], post_ln_b[l, 0])
        if need_ctx:
            mix_h = merge_branches(ym_h, yh_h, yr_h, ih['post'], head_norm_g[l], w_branch[l], w_out[l])
            h = post_norm(ALPHA * h + mod_h[2] * mix_h, post_ln_g[l, 0], post_ln_b[l, 0])
        if l % 2 == 0:
            ffn = functools.partial(swiglu, w_gate=ffn_w_gate[l // 2], w_up=ffn_w_up[l // 2],
                                    w_down=ffn_w_down[l // 2])
        else:
            ffn = functools.partial(moe_swiglu, w_router=moe_w_router[l // 2], w_gate=moe_w_gate[l // 2],
                                    w_up=moe_w_up[l // 2], w_down=moe_w_down[l // 2])
        x = post_norm(ALPHA * x + mod_x[5] * ffn(modulate(x, mod_x[3], mod_x[4])),
                      post_ln_g[l, 1], post_ln_b[l, 1])
        if need_ctx:
            h = post_norm(ALPHA * h + mod_h[5] * ffn(modulate(h, mod_h[3], mod_h[4])),
                          post_ln_g[l, 1], post_ln_b[l, 1])
    return x
```

```python
import functools

import numpy as np
import jax
import jax.numpy as jnp
from jax import lax
from jax.experimental import pallas as pl
from jax.experimental.pallas import tpu as pltpu

F32 = jnp.float32
BF16 = jnp.bfloat16
I32 = jnp.int32

EPS = 1e-6
NEG = -1e30
TINY = 1e-30

LANES = 128
SUBLANES = 8
SCAN_CHUNK = 256
VMEM_LIMIT_BYTES = 56 * 1024 * 1024
GRID_W = 64
MLSTM_HEADS = 4
HGRN_HEADS = 8
RET_HEADS = 4
N_EXPERTS = 8
MOD_ROWS = 16


def _cparams(sem):
    return pltpu.CompilerParams(dimension_semantics=sem, vmem_limit_bytes=VMEM_LIMIT_BYTES)


def _dot(a, b):
    return jnp.dot(a, b, preferred_element_type=F32)


def _dot_nt(a, b):
    return lax.dot_general(a, b, (((1,), (1,)), ((), ())), preferred_element_type=F32)


def _dot_tn(a, b):
    return lax.dot_general(a, b, (((0,), (0,)), ((), ())), preferred_element_type=F32)


def _split_bf16(a):
    hi = a.astype(BF16)
    lo = (a - hi.astype(F32)).astype(BF16)
    return hi, lo


def _silu(a):
    return a * jax.nn.sigmoid(a)


def _ln_rows(a):
    mu = jnp.mean(a, axis=-1, keepdims=True)
    ac = a - mu
    var = jnp.mean(ac * ac, axis=-1, keepdims=True)
    return ac * lax.rsqrt(var + EPS)


def _ada_kernel(cv_ref, w_ref, b_ref, o_ref):
    a = _silu(cv_ref[...])
    a_hi, a_lo = _split_bf16(a)
    w_hi, w_lo = _split_bf16(w_ref[0])
    acc = _dot(a_hi, w_hi) + _dot(a_lo, w_hi) + _dot(a_hi, w_lo)
    o_ref[0] = acc + b_ref[0]


def _ada(cv, w_ada, b_ada):
    n_layers, d, n = w_ada.shape
    tn = 1024
    return pl.pallas_call(
        _ada_kernel,
        out_shape=jax.ShapeDtypeStruct((n_layers, MOD_ROWS, n), F32),
        grid=(n_layers, n // tn),
        in_specs=[pl.BlockSpec((MOD_ROWS, d), lambda l, j: (0, 0)),
                  pl.BlockSpec((1, d, tn), lambda l, j: (l, 0, j)),
                  pl.BlockSpec((1, 1, tn), lambda l, j: (l, 0, j))],
        out_specs=pl.BlockSpec((1, MOD_ROWS, tn), lambda l, j: (l, 0, j)),
        compiler_params=_cparams(("parallel", "parallel")),
        name="ada",
    )(cv, w_ada, b_ada.reshape(n_layers, 1, n))


def _inproj_kernel(x_ref, sh_ref, sc_ref, w_ref, wg_ref, y_ref, g_ref, u_ref):
    j = pl.program_id(2)

    @pl.when(j == 0)
    def _():
        u = _ln_rows(x_ref[0]) * (1.0 + sc_ref[0]) + sh_ref[0]
        ub = u.astype(BF16)
        u_ref[...] = ub
        g_ref[0] = _dot(ub, wg_ref[...])

    y_ref[0] = _dot(u_ref[...], w_ref[...])


def _inproj(xs, mod6, mod_row, w_main, w_gate):
    bsz, s, d = xs.shape
    n = w_main.shape[1]
    tm = min(1024, s)
    tn = 1024
    row = mod_row
    return pl.pallas_call(
        _inproj_kernel,
        out_shape=(jax.ShapeDtypeStruct((bsz, s, n), F32),
                   jax.ShapeDtypeStruct((bsz, s, LANES), F32)),
        grid=(bsz, s // tm, n // tn),
        in_specs=[pl.BlockSpec((1, tm, d), lambda b, i, j: (b, i, 0)),
                  pl.BlockSpec((1, 1, d), lambda b, i, j: (row(b) * 6 + 0, 0, 0)),
                  pl.BlockSpec((1, 1, d), lambda b, i, j: (row(b) * 6 + 1, 0, 0)),
                  pl.BlockSpec((d, tn), lambda b, i, j: (0, j)),
                  pl.BlockSpec((d, LANES), lambda b, i, j: (0, 0))],
        out_specs=(pl.BlockSpec((1, tm, tn), lambda b, i, j: (b, i, j)),
                   pl.BlockSpec((1, tm, LANES), lambda b, i, j: (b, i, 0))),
        scratch_shapes=[pltpu.VMEM((tm, d), BF16)],
        compiler_params=_cparams(("parallel", "parallel", "arbitrary")),
        name="inproj",
    )(xs, mod6, mod6, w_main, w_gate)


_CONV_PAD = 72


def _conv_kernel(a_ref, w_ref, b_ref, o_ref, pad_ref, *, seq, on_grid, k_scale, k_from):
    ct = a_ref.shape[2]
    zeros = jnp.zeros((_CONV_PAD, ct), F32)
    pad_ref[pl.ds(0, _CONV_PAD), :] = zeros
    pad_ref[pl.ds(_CONV_PAD + seq, _CONV_PAD), :] = zeros
    pad_ref[pl.ds(_CONV_PAD, seq), :] = a_ref[0]
    w = w_ref[...]
    bias = b_ref[...]
    scale = jnp.where(pl.program_id(1) >= k_from, k_scale, 1.0).astype(F32)

    rc = min(512, seq)
    for r0 in range(0, seq, rc):
        col = lax.broadcasted_iota(I32, (rc, 1), 0) % GRID_W
        acc = jnp.zeros((rc, ct), F32)
        for dr in range(3) if on_grid else (1,):
            for dc in range(3):
                off = (dr - 1) * GRID_W + (dc - 1)
                tap = pad_ref[pl.ds(_CONV_PAD + r0 + off, rc), :]
                if on_grid and dc == 0:
                    tap = jnp.where(col == 0, 0.0, tap)
                elif on_grid and dc == 2:
                    tap = jnp.where(col == GRID_W - 1, 0.0, tap)
                acc = acc + tap * w[dr * 3 + dc:dr * 3 + dc + 1, :]
        o_ref[0, pl.ds(r0, rc), :] = (_silu(acc + bias) * scale).astype(o_ref.dtype)


def _conv_silu(y, conv_w, conv_b, on_grid, k_scale):
    bsz, seq, _ = y.shape
    ch = conv_w.shape[-1]
    ct = 256
    kern = functools.partial(_conv_kernel, seq=seq, on_grid=on_grid, k_scale=k_scale,
                             k_from=(ch // 2) // ct)
    return pl.pallas_call(
        kern,
        out_shape=jax.ShapeDtypeStruct((bsz, seq, ch), BF16),
        grid=(bsz, ch // ct),
        in_specs=[pl.BlockSpec((1, seq, ct), lambda b, j: (b, 0, j)),
                  pl.BlockSpec((9, ct), lambda b, j: (0, j)),
                  pl.BlockSpec((1, ct), lambda b, j: (0, j))],
        out_specs=pl.BlockSpec((1, seq, ct), lambda b, j: (b, 0, j)),
        scratch_shapes=[pltpu.VMEM((seq + 2 * _CONV_PAD, ct), F32)],
        compiler_params=_cparams(("parallel", "parallel")),
        name="conv_silu",
    )(y, conv_w.reshape(9, ch), conv_b.reshape(1, ch))


def _sched(s, n_lat, n_ctx):
    nc = n_lat + n_ctx
    d = s // nc
    p = s - d * nc
    is_ctx = p < n_ctx
    pc = jnp.minimum(p, n_ctx - 1)
    pq = jnp.maximum(p - n_ctx, 0)
    ctx_c = jnp.where(d == 0, pc, n_ctx - 1 - pc)
    lat_c = jnp.where(d == 0, pq, n_lat - 1 - pq)
    return d, p, is_ctx, lat_c, ctx_c


def _scan_specs(n_lat, n_ctx):
    c = SCAN_CHUNK
    nc = n_lat + n_ctx

    def in_pair(width, col_blk):
        return (pl.BlockSpec((1, c, width), lambda b, h, s: (b, _sched(s, n_lat, n_ctx)[3], col_blk(h, s))),
                pl.BlockSpec((1, c, width), lambda b, h, s: (b, _sched(s, n_lat, n_ctx)[4], col_blk(h, s))))

    def out_pair(width, col_blk):
        return (pl.BlockSpec((1, c, width),
                             lambda b, h, s: (b, _sched(jnp.maximum(s, nc), n_lat, n_ctx)[3], col_blk(h, s))),
                pl.BlockSpec((1, c, width),
                             lambda b, h, s: (b, _sched(jnp.maximum(s, nc), n_lat, n_ctx)[4], col_blk(h, s))))

    return in_pair, out_pair


def _store_scan_out(is_ctx, val, ol_ref, oc_ref):
    @pl.when(is_ctx)
    def _():
        oc_ref[0] = val.astype(oc_ref.dtype)

    @pl.when(jnp.logical_not(is_ctx))
    def _():
        ol_ref[0] = val.astype(ol_ref.dtype)


def _log_sigmoid(a):
    return jnp.minimum(a, 0.0) - jnp.log1p(jnp.exp(-jnp.abs(a)))


def _prefix_sums(vals_col, vals_row, tri_b):
    lane = lax.broadcasted_iota(I32, (1, LANES), 1)
    hi = vals_col.astype(BF16).astype(F32)
    rhs = jnp.where(lane == 0, hi, jnp.where(lane == 1, vals_col - hi, 0.0)).astype(BF16)
    bc = _dot(tri_b, rhs)
    col = bc[:, 0:1] + bc[:, 1:2]
    sub = lax.broadcasted_iota(I32, (2 * SUBLANES, 1), 0)
    rhi = vals_row.astype(BF16).astype(F32)
    lhs = jnp.where(sub == 0, rhi, jnp.where(sub == 1, vals_row - rhi, 0.0)).astype(BF16)
    br = _dot_nt(lhs, tri_b)
    row = br[0:1, :] + br[1:2, :]
    return col, row


def _mlstm_kernel(ql_ref, qc_ref, kl_ref, kc_ref, vl_ref, vc_ref, zl_ref, zc_ref, gl_ref, gc_ref,
                  gb_ref, ng_ref, ol_ref, oc_ref, c_ref, n_ref, m_ref, hbuf_ref, *, n_lat, n_ctx):
    h = pl.program_id(1)
    d, p, is_ctx, lat_c, ctx_c = _sched(pl.program_id(2), n_lat, n_ctx)
    chunk = jnp.where(is_ctx, n_lat + ctx_c, lat_c)
    pick = lambda a_l, a_c: jnp.where(is_ctx, a_c[0], a_l[0])
    c = SCAN_CHUNK

    @pl.when(p == 0)
    def _():
        c_ref[...] = jnp.zeros_like(c_ref)
        n_ref[...] = jnp.zeros_like(n_ref)
        m_ref[...] = jnp.full_like(m_ref, NEG)

    gates = pick(gl_ref, gc_ref) + gb_ref[...]
    lane = lax.broadcasted_iota(I32, (1, LANES), 1)
    ci = (2 * d) * MLSTM_HEADS + h
    cf = (2 * d + 1) * MLSTM_HEADS + h
    li = jnp.sum(jnp.where(lane == ci, gates, 0.0), axis=1, keepdims=True)
    lf = _log_sigmoid(jnp.sum(jnp.where(lane == cf, gates, 0.0), axis=1, keepdims=True))
    r32 = jnp.where(lane == 0, lf, jnp.where(lane == 1, li, 0.0))
    rt = r32.T
    lf_row = rt[0:1, :]
    li_row = rt[1:2, :]

    t_i = lax.broadcasted_iota(I32, (c, c), 0)
    s_i = lax.broadcasted_iota(I32, (c, c), 1)
    tri = (t_i - s_i) * (1 - 2 * d) >= 0
    b_col, b_row = _prefix_sums(lf, lf_row, tri.astype(BF16))

    m_prev = m_ref[0:1, 0:1]
    d_mat = jnp.where(tri, b_col - b_row + li_row, NEG)
    a_inter = b_col + m_prev
    m_t = jnp.maximum(a_inter, jnp.max(d_mat, axis=1, keepdims=True))
    w_inter = jnp.exp(a_inter - m_t)
    q = pick(ql_ref, qc_ref)
    k = pick(kl_ref, kc_ref)
    vb = pick(vl_ref, vc_ref).astype(BF16)
    s_mat = _dot_nt(q, k) * jnp.exp(d_mat - m_t)
    num = w_inter * _dot(q, c_ref[...].astype(BF16)) + _dot(s_mat.astype(BF16), vb)
    den = (w_inter * jnp.sum(q.astype(F32) * n_ref[0:1, :], axis=1, keepdims=True)
           + jnp.sum(s_mat, axis=1, keepdims=True))
    hh = num / jnp.maximum(jnp.abs(den), jnp.exp(-m_t))

    total = jnp.sum(lf, axis=0, keepdims=True)
    g_col = total - b_col + li
    m_new = jnp.maximum(total + m_prev, jnp.max(g_col, axis=0, keepdims=True))
    decay = jnp.exp(total + m_prev - m_new)
    kw = k.astype(F32) * jnp.exp(g_col - m_new)
    c_ref[...] = decay * c_ref[...] + _dot_tn(kw.astype(BF16), vb)
    n_new = decay * n_ref[0:1, :] + jnp.sum(kw, axis=0, keepdims=True)
    n_ref[...] = jnp.broadcast_to(n_new, n_ref.shape)
    m_ref[...] = jnp.broadcast_to(m_new, m_ref.shape)

    @pl.when(d == 0)
    def _():
        hbuf_ref[chunk] = hh

    @pl.when(d == 1)
    def _():
        hn = _ln_rows(hbuf_ref[chunk] + hh)
        _store_scan_out(is_ctx, hn * ng_ref[...] * _silu(pick(zl_ref, zc_ref)), ol_ref, oc_ref)


def _mlstm(qk, y, g, gate_b, norm_g, v_col, z_col):
    bsz, s_lat, w2 = qk[0].shape
    s_ctx = qk[1].shape[1]
    width = w2 // 2
    dh = width // MLSTM_HEADS
    c = SCAN_CHUNK
    n_lat, n_ctx = s_lat // c, s_ctx // c
    in_pair, out_pair = _scan_specs(n_lat, n_ctx)
    kern = functools.partial(_mlstm_kernel, n_lat=n_lat, n_ctx=n_ctx)
    gb = jnp.zeros((1, LANES), F32).at[0, :gate_b.shape[0]].set(gate_b)
    return pl.pallas_call(
        kern,
        out_shape=(jax.ShapeDtypeStruct((bsz, s_lat, width), BF16),
                   jax.ShapeDtypeStruct((bsz, s_ctx, width), BF16)),
        grid=(bsz, MLSTM_HEADS, 2 * (n_lat + n_ctx)),
        in_specs=[*in_pair(dh, lambda h, s: h),
                  *in_pair(dh, lambda h, s: MLSTM_HEADS + h),
                  *in_pair(dh, lambda h, s: v_col // dh + h),
                  *out_pair(dh, lambda h, s: z_col // dh + h),
                  *in_pair(LANES, lambda h, s: 0),
                  pl.BlockSpec((1, LANES), lambda b, h, s: (0, 0)),
                  pl.BlockSpec((1, dh), lambda b, h, s: (0, h))],
        out_specs=out_pair(dh, lambda h, s: h),
        scratch_shapes=[pltpu.VMEM((dh, dh), F32),
                        pltpu.VMEM((SUBLANES, dh), F32),
                        pltpu.VMEM((SUBLANES, LANES), F32),
                        pltpu.VMEM((n_lat + n_ctx, c, dh), F32)],
        compiler_params=_cparams(("parallel", "parallel", "arbitrary")),
        name="mlstm",
    )(qk[0], qk[1], qk[0], qk[1], y[0], y[1], y[0], y[1], g[0], g[1], gb, norm_g.reshape(1, width))


def _ret_kernel(ql_ref, qc_ref, kl_ref, kc_ref, vl_ref, vc_ref, zl_ref, zc_ref, dl_ref, ng_ref,
                ol_ref, oc_ref, s_ref, hbuf_ref, *, n_lat, n_ctx, q_scale):
    h = pl.program_id(1)
    d, p, is_ctx, lat_c, ctx_c = _sched(pl.program_id(2), n_lat, n_ctx)
    chunk = jnp.where(is_ctx, n_lat + ctx_c, lat_c)
    pick = lambda a_l, a_c: jnp.where(is_ctx, a_c[0], a_l[0])
    c = SCAN_CHUNK

    @pl.when(p == 0)
    def _():
        s_ref[...] = jnp.zeros_like(s_ref)

    lane = lax.broadcasted_iota(I32, (1, LANES), 1)
    logit = jnp.sum(jnp.where(lane == d * RET_HEADS + h, dl_ref[...], 0.0), axis=1, keepdims=True)
    lg = _log_sigmoid(logit)

    t_i = lax.broadcasted_iota(I32, (c, c), 0)
    s_i = lax.broadcasted_iota(I32, (c, c), 1)
    dist = jnp.where(d == 0, t_i - s_i, s_i - t_i)
    dmat = jnp.where(dist >= 0, jnp.exp(lg * jnp.maximum(dist, 0).astype(F32)), 0.0)
    t_c = lax.broadcasted_iota(I32, (c, 1), 0)
    pos = jnp.where(d == 0, t_c + 1, c - t_c).astype(F32)
    inter_w = jnp.exp(lg * pos)
    state_w = jnp.exp(lg * (c - pos))
    chunk_decay = jnp.exp(lg * c)

    qb = (pick(ql_ref, qc_ref) * q_scale).astype(BF16)
    kf = pick(kl_ref, kc_ref)
    vb = pick(vl_ref, vc_ref).astype(BF16)
    a = _dot_nt(qb, kf.astype(BF16)) * dmat
    o = inter_w * _dot(qb, s_ref[...].astype(BF16)) + _dot(a.astype(BF16), vb)
    s_ref[...] = chunk_decay * s_ref[...] + _dot_tn((kf * state_w).astype(BF16), vb)

    @pl.when(d == 0)
    def _():
        hbuf_ref[chunk] = o

    @pl.when(d == 1)
    def _():
        hn = _ln_rows(hbuf_ref[chunk] + o)
        _store_scan_out(is_ctx, hn * ng_ref[...] * _silu(pick(zl_ref, zc_ref)), ol_ref, oc_ref)


def _retention(y, decay_logit, norm_g, q_col, k_col, v_col, z_col, dk, dv):
    bsz, s_lat, _ = y[0].shape
    s_ctx = y[1].shape[1]
    c = SCAN_CHUNK
    n_lat, n_ctx = s_lat // c, s_ctx // c
    in_pair, out_pair = _scan_specs(n_lat, n_ctx)
    kern = functools.partial(_ret_kernel, n_lat=n_lat, n_ctx=n_ctx, q_scale=float(dk) ** -0.5)
    dl = jnp.zeros((1, LANES), F32).at[0, :2 * RET_HEADS].set(decay_logit.reshape(-1))
    return pl.pallas_call(
        kern,
        out_shape=(jax.ShapeDtypeStruct((bsz, s_lat, RET_HEADS * dv), BF16),
                   jax.ShapeDtypeStruct((bsz, s_ctx, RET_HEADS * dv), BF16)),
        grid=(bsz, RET_HEADS, 2 * (n_lat + n_ctx)),
        in_specs=[*in_pair(dk, lambda h, s: q_col // dk + h),
                  *in_pair(dk, lambda h, s: k_col // dk + h),
                  *in_pair(dv, lambda h, s: v_col // dv + h),
                  *out_pair(dv, lambda h, s: z_col // dv + h),
                  pl.BlockSpec((1, LANES), lambda b, h, s: (0, 0)),
                  pl.BlockSpec((1, dv), lambda b, h, s: (0, h))],
        out_specs=out_pair(dv, lambda h, s: h),
        scratch_shapes=[pltpu.VMEM((dk, dv), F32),
                        pltpu.VMEM((n_lat + n_ctx, c, dv), F32)],
        compiler_params=_cparams(("parallel", "parallel", "arbitrary")),
        name="retention",
    )(y[0], y[1], y[0], y[1], y[0], y[1], y[0], y[1], dl, norm_g.reshape(1, RET_HEADS * dv))


def _hgrn_tables(c):
    nl = int(np.log2(c))
    t = np.arange(c)[:, None]
    u = np.arange(c)[None, :]
    tabs = np.zeros((2, nl + 2, c, c), np.float32)
    for lev in range(nl):
        n = 1 << lev
        same = (t // n) == (u // n)
        second = ((t // n) % 2) == 1
        tabs[0, lev] = np.where(second, same & (u <= t), same & (u > t))
        tabs[1, lev] = np.where(second, same & (u < t), same & (u >= t))
    tabs[0, nl] = u <= t
    tabs[0, nl + 1] = u > t
    tabs[1, nl] = u >= t
    tabs[1, nl + 1] = u < t
    x = (t ^ u).astype(np.int64)
    lvl = np.where(x == 0, nl, np.floor(np.log2(np.maximum(x, 1))).astype(np.int64))
    return (jnp.asarray(tabs.reshape(2, (nl + 2) * c, c), BF16), jnp.asarray(lvl, I32), nl)


def _hgrn_kernel(ql_ref, qc_ref, vl_ref, vc_ref, fl_ref, fc_ref, zl_ref, zc_ref, lb_ref, ng_ref,
                 tab_ref, lvl_ref, ol_ref, oc_ref, st_ref, hbuf_ref, *, n_lat, n_ctx, layer, nl):
    d, p, is_ctx, lat_c, ctx_c = _sched(pl.program_id(2), n_lat, n_ctx)
    chunk = jnp.where(is_ctx, n_lat + ctx_c, lat_c)
    pick = lambda a_l, a_c: jnp.where(is_ctx, a_c[0], a_l[0])
    c = SCAN_CHUNK

    @pl.when(p == 0)
    def _():
        st_ref[...] = jnp.zeros_like(st_ref)

    lbp = lb_ref[0]
    sm = jnp.exp(lbp - jnp.max(lbp, axis=0, keepdims=True))
    sm = sm / jnp.sum(sm, axis=0, keepdims=True)
    lb = jnp.zeros((1, lbp.shape[1]), F32)
    for i in range(1, layer + 1):
        lb = lb + sm[i:i + 1, :]

    ft = pick(fl_ref, fc_ref)
    kk = (1.0 - lb) * jax.nn.sigmoid(-ft)
    lf = jnp.log(jnp.maximum(lb + (1.0 - lb) * jax.nn.sigmoid(ft), TINY))
    q = _silu(pick(ql_ref, qc_ref))
    vb = pick(vl_ref, vc_ref).astype(BF16)
    lf_hi, lf_lo = _split_bf16(lf)
    rhs = jnp.concatenate([lf_hi, lf_lo], axis=1)
    dk = lf.shape[1]

    def seg_exp(idx):
        ps = _dot(tab_ref[0, pl.ds(idx * c, c), :], rhs)
        return jnp.exp(ps[:, :dk] + ps[:, dk:])

    t_c = lax.broadcasted_iota(I32, (c, 1), 0)
    lvl = lvl_ref[...]
    a = jnp.where(lvl == nl, _dot_nt(q.astype(BF16), kk.astype(BF16)), 0.0)
    for lev in range(nl):
        e = seg_exp(lev)
        is_q = ((t_c >> lev) & 1) != d
        qt = jnp.where(is_q, q * e, 0.0).astype(BF16)
        kt = jnp.where(is_q, 0.0, kk * e).astype(BF16)
        a = jnp.where(lvl == lev, _dot_nt(qt, kt), a)

    st = st_ref[...]
    o = _dot(a.astype(BF16), vb) + _dot_nt((q * seg_exp(nl)).astype(BF16), st.astype(BF16))
    total = jnp.sum(lf, axis=0, keepdims=True)
    st_ref[...] = st * jnp.exp(total) + _dot_tn(vb, (kk * seg_exp(nl + 1)).astype(BF16))

    @pl.when(d == 0)
    def _():
        hbuf_ref[chunk] = o

    @pl.when(d == 1)
    def _():
        tot = hbuf_ref[chunk] + o
        hn = tot * lax.rsqrt(jnp.mean(tot * tot, axis=-1, keepdims=True) + EPS)
        _store_scan_out(is_ctx, hn * ng_ref[...] * _silu(pick(zl_ref, zc_ref)), ol_ref, oc_ref)


def _hgrn(y, hgrn_lb, layer, norm_g, q_col, v_col, f_col, z_col, width):
    bsz, s_lat, _ = y[0].shape
    s_ctx = y[1].shape[1]
    dk = width // HGRN_HEADS
    depth = hgrn_lb.shape[1]
    c = SCAN_CHUNK
    n_lat, n_ctx = s_lat // c, s_ctx // c
    nc = n_lat + n_ctx
    tabs, lvl, nl = _hgrn_tables(c)
    in_pair, out_pair = _scan_specs(n_lat, n_ctx)
    kern = functools.partial(_hgrn_kernel, n_lat=n_lat, n_ctx=n_ctx, layer=layer, nl=nl)
    per_dir = width // dk
    return pl.pallas_call(
        kern,
        out_shape=(jax.ShapeDtypeStruct((bsz, s_lat, width), BF16),
                   jax.ShapeDtypeStruct((bsz, s_ctx, width), BF16)),
        grid=(bsz, HGRN_HEADS, 2 * nc),
        in_specs=[*in_pair(dk, lambda h, s: q_col // dk + h),
                  *in_pair(dk, lambda h, s: v_col // dk + h),
                  *in_pair(dk, lambda h, s: f_col // dk + (s // nc) * per_dir + h),
                  *out_pair(dk, lambda h, s: z_col // dk + h),
                  pl.BlockSpec((1, depth, dk), lambda b, h, s: (s // nc, 0, h)),
                  pl.BlockSpec((1, dk), lambda b, h, s: (0, h)),
                  pl.BlockSpec((1, (nl + 2) * c, c), lambda b, h, s: (s // nc, 0, 0)),
                  pl.BlockSpec((c, c), lambda b, h, s: (0, 0))],
        out_specs=out_pair(dk, lambda h, s: h),
        scratch_shapes=[pltpu.VMEM((dk, dk), F32),
                        pltpu.VMEM((nc, c, dk), F32)],
        compiler_params=_cparams(("parallel", "parallel", "arbitrary")),
        name="hgrn2",
    )(y[0], y[1], y[0], y[1], y[0], y[1], y[0], y[1], hgrn_lb, norm_g.reshape(1, width), tabs, lvl)


def _merge_kernel(bm_ref, bh_ref, br_ref, g0_ref, g1_ref, g2_ref, w_ref, o_ref):
    acc = jax.nn.sigmoid(g0_ref[0]) * _dot(bm_ref[0], w_ref[0])
    acc = acc + jax.nn.sigmoid(g1_ref[0]) * _dot(bh_ref[0], w_ref[1])
    acc = acc + jax.nn.sigmoid(g2_ref[0]) * _dot(br_ref[0], w_ref[2])
    o_ref[0] = acc.astype(o_ref.dtype)


def _merge_proj(br_m, br_h, br_r, y, w_br, merge_col):
    bsz, s, width = br_m.shape
    d = w_br.shape[2]
    tm = min(1024, s)
    tn = 512
    gcol = merge_col // tn
    nj = d // tn
    bspec = pl.BlockSpec((1, tm, width), lambda b, i, j: (b, i, 0))

    def gspec(n):
        return pl.BlockSpec((1, tm, tn), lambda b, i, j: (b, i, gcol + n * nj + j))

    return pl.pallas_call(
        _merge_kernel,
        out_shape=jax.ShapeDtypeStruct((bsz, s, d), BF16),
        grid=(bsz, s // tm, nj),
        in_specs=[bspec, bspec, bspec, gspec(0), gspec(1), gspec(2),
                  pl.BlockSpec((3, width, tn), lambda b, i, j: (0, 0, j))],
        out_specs=pl.BlockSpec((1, tm, tn), lambda b, i, j: (b, i, j)),
        compiler_params=_cparams(("parallel", "parallel", "arbitrary")),
        name="merge_proj",
    )(br_m, br_h, br_r, y, y, y, w_br)


def _down_kernel(a_ref, w_ref, x_ref, gate_ref, g_ref, b_ref, o_ref, acc_ref, *, alpha, nk):
    k = pl.program_id(2)

    @pl.when(k == 0)
    def _():
        acc_ref[...] = jnp.zeros_like(acc_ref)

    acc_ref[...] += _dot(a_ref[0], w_ref[...])

    @pl.when(k == nk - 1)
    def _():
        z = alpha * x_ref[0] + gate_ref[0] * acc_ref[...]
        o_ref[0] = _ln_rows(z) * g_ref[...] + b_ref[...]


def _down(a, w, xs, mod6, mod_row, which, ln_g, ln_b, alpha):
    bsz, s, d = xs.shape
    kdim = w.shape[0]
    tm = min(1024, s)
    tk = 512 if kdim % 512 == 0 else kdim
    nk = kdim // tk
    row = mod_row
    kern = functools.partial(_down_kernel, alpha=alpha, nk=nk)
    return pl.pallas_call(
        kern,
        out_shape=jax.ShapeDtypeStruct((bsz, s, d), F32),
        grid=(bsz, s // tm, nk),
        in_specs=[pl.BlockSpec((1, tm, tk), lambda b, i, k: (b, i, k)),
                  pl.BlockSpec((tk, d), lambda b, i, k: (k, 0)),
                  pl.BlockSpec((1, tm, d), lambda b, i, k: (b, i, 0)),
                  pl.BlockSpec((1, 1, d), lambda b, i, k: (row(b) * 6 + which, 0, 0)),
                  pl.BlockSpec((1, d), lambda b, i, k: (0, 0)),
                  pl.BlockSpec((1, d), lambda b, i, k: (0, 0))],
        out_specs=pl.BlockSpec((1, tm, d), lambda b, i, k: (b, i, 0)),
        scratch_shapes=[pltpu.VMEM((tm, d), F32)],
        compiler_params=_cparams(("parallel", "parallel", "arbitrary")),
        name="down_postnorm",
    )(a, w, xs, mod6, ln_g.reshape(1, d), ln_b.reshape(1, d))


def _ffn_up_kernel(x_ref, sh_ref, sc_ref, wg_ref, wu_ref, h_ref, u_ref):
    @pl.when(pl.program_id(2) == 0)
    def _():
        u = _ln_rows(x_ref[0]) * (1.0 + sc_ref[0]) + sh_ref[0]
        u_ref[...] = u.astype(BF16)

    ub = u_ref[...]
    h_ref[0] = (_silu(_dot(ub, wg_ref[...])) * _dot(ub, wu_ref[...])).astype(h_ref.dtype)


def _ffn_up(xs, mod6, mod_row, w_gate, w_up):
    bsz, s, d = xs.shape
    f = w_gate.shape[1]
    tm = min(1024, s)
    tn = 512
    row = mod_row
    return pl.pallas_call(
        _ffn_up_kernel,
        out_shape=jax.ShapeDtypeStruct((bsz, s, f), BF16),
        grid=(bsz, s // tm, f // tn),
        in_specs=[pl.BlockSpec((1, tm, d), lambda b, i, j: (b, i, 0)),
                  pl.BlockSpec((1, 1, d), lambda b, i, j: (row(b) * 6 + 3, 0, 0)),
                  pl.BlockSpec((1, 1, d), lambda b, i, j: (row(b) * 6 + 4, 0, 0)),
                  pl.BlockSpec((d, tn), lambda b, i, j: (0, j)),
                  pl.BlockSpec((d, tn), lambda b, i, j: (0, j))],
        out_specs=pl.BlockSpec((1, tm, tn), lambda b, i, j: (b, i, j)),
        scratch_shapes=[pltpu.VMEM((tm, d), BF16)],
        compiler_params=_cparams(("parallel", "parallel", "arbitrary")),
        name="ffn_up",
    )(xs, mod6, mod6, w_gate, w_up)


MOE_ROW_TILE = 512


def _router_kernel(x_ref, sh_ref, sc_ref, w_ref, u_ref, rt_ref, tot_ref, carry_ref, *, n_exp):
    i = pl.program_id(0)

    @pl.when(i == 0)
    def _():
        carry_ref[...] = jnp.zeros_like(carry_ref)

    u = _ln_rows(x_ref[...]) * (1.0 + sc_ref[0]) + sh_ref[0]
    u_ref[...] = u
    u_hi, u_lo = _split_bf16(u)
    w_hi, w_lo = _split_bf16(w_ref[...])
    logits = _dot(u_hi, w_hi) + _dot(u_lo, w_hi) + _dot(u_hi, w_lo)
    tr = logits.shape[0]
    lane = lax.broadcasted_iota(I32, (1, LANES), 1)
    ninf = jnp.float32(-jnp.inf)
    lg = jnp.where(lane < n_exp, logits, ninf)
    m1 = jnp.max(lg, axis=1, keepdims=True)
    i1 = jnp.min(jnp.where(lg == m1, lane, LANES), axis=1, keepdims=True)
    lg2 = jnp.where(lane == i1, ninf, lg)
    m2 = jnp.max(lg2, axis=1, keepdims=True)
    i2 = jnp.min(jnp.where(lg2 == m2, lane, LANES), axis=1, keepdims=True)
    e = jnp.exp(m2 - m1)
    w0 = 1.0 / (1.0 + e)
    w1 = e / (1.0 + e)

    oh = ((lane == i1) | (lane == i2)).astype(F32)
    r_i = lax.broadcasted_iota(I32, (tr, tr), 0)
    c_i = lax.broadcasted_iota(I32, (tr, tr), 1)
    cum = _dot((c_i < r_i).astype(BF16), oh.astype(BF16)) + carry_ref[0:1, :]
    r0 = jnp.sum(jnp.where(lane == i1, cum, 0.0), axis=1, keepdims=True)
    r1 = jnp.sum(jnp.where(lane == i2, cum, 0.0), axis=1, keepdims=True)
    new_carry = carry_ref[0:1, :] + jnp.sum(oh, axis=0, keepdims=True)
    carry_ref[...] = jnp.broadcast_to(new_carry, carry_ref.shape)
    tot_ref[...] = jnp.broadcast_to(new_carry, tot_ref.shape)

    rt = jnp.where(lane == 0, i1.astype(F32), 0.0)
    rt = jnp.where(lane == 1, i2.astype(F32), rt)
    rt = jnp.where(lane == 2, w0, rt)
    rt = jnp.where(lane == 3, w1, rt)
    rt = jnp.where(lane == 4, r0, rt)
    rt = jnp.where(lane == 5, r1, rt)
    rt_ref[...] = rt


def _router(x2, mod6, mod_row, seq, w_router):
    t, d = x2.shape
    n_exp = w_router.shape[1]
    tr = min(512, seq)
    wr = jnp.zeros((d, LANES), F32).at[:, :n_exp].set(w_router)
    kern = functools.partial(_router_kernel, n_exp=n_exp)
    per = seq // tr
    return pl.pallas_call(
        kern,
        out_shape=(jax.ShapeDtypeStruct((t, d), F32),
                   jax.ShapeDtypeStruct((t, LANES), F32),
                   jax.ShapeDtypeStruct((SUBLANES, LANES), F32)),
        grid=(t // tr,),
        in_specs=[pl.BlockSpec((tr, d), lambda i: (i, 0)),
                  pl.BlockSpec((1, 1, d), lambda i: (mod_row(i // per) * 6 + 3, 0, 0)),
                  pl.BlockSpec((1, 1, d), lambda i: (mod_row(i // per) * 6 + 4, 0, 0)),
                  pl.BlockSpec((d, LANES), lambda i: (0, 0))],
        out_specs=(pl.BlockSpec((tr, d), lambda i: (i, 0)),
                   pl.BlockSpec((tr, LANES), lambda i: (i, 0)),
                   pl.BlockSpec((SUBLANES, LANES), lambda i: (0, 0))),
        scratch_shapes=[pltpu.VMEM((SUBLANES, LANES), F32)],
        compiler_params=_cparams(("arbitrary",)),
        name="moe_router",
    )(x2, mod6, mod6, wr)


def _dispatch_kernel(tot_ref, e0_ref, e1_ref, r0_ref, r1_ref, u_hbm,
                     xs_hbm, d0_ref, d1_ref, be_ref, gs_ref, zero_ref, sem, zsem,
                     *, n_exp, n_blocks, tile, td):
    i = pl.program_id(0)

    @pl.when(i == 0)
    def _():
        start = jnp.int32(0)
        for e in range(n_exp):
            gs_ref[e] = start
            start = start + ((tot_ref[e] + (tile - 1)) // tile) * tile
        gs_ref[n_exp] = start

        def blk(b, carry):
            row = b * tile
            ex = jnp.int32(0)
            for e in range(1, n_exp):
                ex = jnp.where(row >= gs_ref[e], e, ex)
            be_ref[0, b] = ex
            be_ref[1, b] = jnp.where(row < gs_ref[n_exp], 1, 0).astype(I32)
            return carry

        lax.fori_loop(0, n_blocks, blk, 0)

        zero_ref[...] = jnp.zeros_like(zero_ref)

        def zstart(b, carry):
            pltpu.make_async_copy(zero_ref, xs_hbm.at[pl.ds(b * tile, tile)], zsem).start()
            return carry

        def zwait(b, carry):
            pltpu.make_async_copy(zero_ref, xs_hbm.at[pl.ds(0, tile)], zsem).wait()
            return carry

        lax.fori_loop(0, n_blocks, zstart, 0)
        lax.fori_loop(0, n_blocks, zwait, 0)

    def row(t, carry):
        src = i * td + t
        da = gs_ref[e0_ref[0, 0, t]] + r0_ref[0, 0, t]
        db = gs_ref[e1_ref[0, 0, t]] + r1_ref[0, 0, t]
        d0_ref[0, 0, t] = da
        d1_ref[0, 0, t] = db
        pltpu.make_async_copy(u_hbm.at[pl.ds(src, 1)], xs_hbm.at[pl.ds(da, 1)], sem).start()
        pltpu.make_async_copy(u_hbm.at[pl.ds(src, 1)], xs_hbm.at[pl.ds(db, 1)], sem).start()
        return carry

    def row_wait(t, carry):
        pltpu.make_async_copy(u_hbm.at[pl.ds(0, 1)], xs_hbm.at[pl.ds(0, 1)], sem).wait()
        return carry

    lax.fori_loop(0, td, row, 0)
    lax.fori_loop(0, 2 * td, row_wait, 0)


def _dispatch(tot_i, e0, e1, r0, r1, u2, n_blocks, tile):
    t, d = u2.shape
    td = min(512, t)
    n_exp = N_EXPERTS
    nt = t // td
    resh = lambda a: a.reshape(nt, 1, td)
    smem_blk = pl.BlockSpec((1, 1, td), lambda i, tot: (i, 0, 0), memory_space=pltpu.SMEM)
    kern = functools.partial(_dispatch_kernel, n_exp=n_exp, n_blocks=n_blocks, tile=tile, td=td)
    grid_spec = pltpu.PrefetchScalarGridSpec(
        num_scalar_prefetch=1,
        grid=(nt,),
        in_specs=[smem_blk, smem_blk, smem_blk, smem_blk, pl.BlockSpec(memory_space=pl.ANY)],
        out_specs=(pl.BlockSpec(memory_space=pl.ANY), smem_blk, smem_blk,
                   pl.BlockSpec((2, n_blocks), lambda i, tot: (0, 0), memory_space=pltpu.SMEM)),
        scratch_shapes=[pltpu.SMEM((n_exp + 1,), I32),
                        pltpu.VMEM((tile, d), F32),
                        pltpu.SemaphoreType.DMA(()),
                        pltpu.SemaphoreType.DMA(())],
    )
    xs, d0, d1, be = pl.pallas_call(
        kern,
        out_shape=(jax.ShapeDtypeStruct((n_blocks * tile, d), F32),
                   jax.ShapeDtypeStruct((nt, 1, td), I32),
                   jax.ShapeDtypeStruct((nt, 1, td), I32),
                   jax.ShapeDtypeStruct((2, n_blocks), I32)),
        grid_spec=grid_spec,
        compiler_params=_cparams(("arbitrary",)),
        name="moe_dispatch",
    )(tot_i, resh(e0), resh(e1), resh(r0), resh(r1), u2)
    return xs, d0.reshape(t), d1.reshape(t), be


def _expert_kernel(be_ref, x_ref, wg_ref, wu_ref, wd_ref, y_ref, xb_ref, acc_ref, *, nf):
    b = pl.program_id(0)
    f = pl.program_id(1)
    valid = be_ref[1, b] == 1

    @pl.when(valid)
    def _():
        @pl.when(f == 0)
        def _():
            xb_ref[...] = x_ref[...].astype(BF16)
            acc_ref[...] = jnp.zeros_like(acc_ref)

        xb = xb_ref[...]
        hmid = _silu(_dot(xb, wg_ref[0])) * _dot(xb, wu_ref[0])
        acc_ref[...] += _dot(hmid.astype(BF16), wd_ref[0])

        @pl.when(f == nf - 1)
        def _():
            y_ref[...] = acc_ref[...]

    @pl.when(jnp.logical_not(valid) & (f == 0))
    def _():
        y_ref[...] = jnp.zeros_like(y_ref)


def _expert_ffn(be, xs, w_gate, w_up, w_down, tile):
    rows, d = xs.shape
    n_blocks = rows // tile
    fdim = w_gate.shape[2]
    tf = 512
    nf = fdim // tf

    def fsel(b, f, be_ref):
        return jnp.where(be_ref[1, b] == 1, f, nf - 1)

    grid_spec = pltpu.PrefetchScalarGridSpec(
        num_scalar_prefetch=1,
        grid=(n_blocks, nf),
        in_specs=[pl.BlockSpec((tile, d), lambda b, f, be_ref: (b, 0)),
                  pl.BlockSpec((1, d, tf), lambda b, f, be_ref: (be_ref[0, b], 0, fsel(b, f, be_ref))),
                  pl.BlockSpec((1, d, tf), lambda b, f, be_ref: (be_ref[0, b], 0, fsel(b, f, be_ref))),
                  pl.BlockSpec((1, tf, d), lambda b, f, be_ref: (be_ref[0, b], fsel(b, f, be_ref), 0))],
        out_specs=pl.BlockSpec((tile, d), lambda b, f, be_ref: (b, 0)),
        scratch_shapes=[pltpu.VMEM((tile, d), BF16), pltpu.VMEM((tile, d), F32)],
    )
    return pl.pallas_call(
        functools.partial(_expert_kernel, nf=nf),
        out_shape=jax.ShapeDtypeStruct((rows, d), F32),
        grid_spec=grid_spec,
        compiler_params=_cparams(("parallel", "arbitrary")),
        name="moe_expert_ffn",
    )(be, xs, w_gate, w_up, w_down)


def _combine_kernel(d0_ref, d1_ref, rt_ref, x_ref, gate_ref, g_ref, b_ref, ys_hbm, o_ref,
                    buf_ref, sem, *, alpha, tc):
    def start(t, carry):
        pltpu.make_async_copy(ys_hbm.at[pl.ds(d0_ref[0, 0, t], 1)], buf_ref.at[0, pl.ds(t, 1)], sem).start()
        pltpu.make_async_copy(ys_hbm.at[pl.ds(d1_ref[0, 0, t], 1)], buf_ref.at[1, pl.ds(t, 1)], sem).start()
        return carry

    def wait(t, carry):
        pltpu.make_async_copy(ys_hbm.at[pl.ds(0, 1)], buf_ref.at[0, pl.ds(0, 1)], sem).wait()
        return carry

    lax.fori_loop(0, tc, start, 0)
    lax.fori_loop(0, 2 * tc, wait, 0)
    rt = rt_ref[...]
    ffn = rt[:, 2:3] * buf_ref[0] + rt[:, 3:4] * buf_ref[1]
    z = alpha * x_ref[...] + gate_ref[0] * ffn
    o_ref[...] = _ln_rows(z) * g_ref[...] + b_ref[...]


def _combine(d0, d1, rt, x2, mod6, mod_row, seq, ln_g, ln_b, ys, alpha):
    t, d = x2.shape
    tc = min(256, seq)
    nt = t // tc
    per = seq // tc
    smem_blk = pl.BlockSpec((1, 1, tc), lambda i: (i, 0, 0), memory_space=pltpu.SMEM)
    return pl.pallas_call(
        functools.partial(_combine_kernel, alpha=alpha, tc=tc),
        out_shape=jax.ShapeDtypeStruct((t, d), F32),
        grid=(nt,),
        in_specs=[smem_blk, smem_blk,
                  pl.BlockSpec((tc, LANES), lambda i: (i, 0)),
                  pl.BlockSpec((tc, d), lambda i: (i, 0)),
                  pl.BlockSpec((1, 1, d), lambda i: (mod_row(i // per) * 6 + 5, 0, 0)),
                  pl.BlockSpec((1, d), lambda i: (0, 0)),
                  pl.BlockSpec((1, d), lambda i: (0, 0)),
                  pl.BlockSpec(memory_space=pl.ANY)],
        out_specs=pl.BlockSpec((tc, d), lambda i: (i, 0)),
        scratch_shapes=[pltpu.VMEM((2, tc, d), F32), pltpu.SemaphoreType.DMA(())],
        compiler_params=_cparams(("arbitrary",)),
        name="moe_combine",
    )(d0.reshape(nt, 1, tc), d1.reshape(nt, 1, tc), rt, x2, mod6,
      ln_g.reshape(1, d), ln_b.reshape(1, d), ys)


def _moe_layer(xs, mod6, mod_row, w_router, w_gate, w_up, w_down, ln_g, ln_b, alpha):
    bsz, s, d = xs.shape
    t = bsz * s
    x2 = xs.reshape(t, d)
    tile = min(MOE_ROW_TILE, t)
    n_blocks = (2 * t) // tile + N_EXPERTS
    u2, rt, tot = _router(x2, mod6, mod_row, s, w_router)
    ri = rt[:, :8].astype(I32)
    tot_i = tot[0].astype(I32)
    xs_sorted, d0, d1, be = _dispatch(tot_i, ri[:, 0], ri[:, 1], ri[:, 4], ri[:, 5], u2, n_blocks, tile)
    ys = _expert_ffn(be, xs_sorted, w_gate, w_up, w_down, tile)
    out = _combine(d0, d1, rt, x2, mod6, mod_row, s, ln_g, ln_b, ys, alpha)
    return out.reshape(bsz, s, d)


def kernel(x, c, ctx, c_ctx, w_ada, b_ada, w_in, conv_w, conv_b, mlstm_gate_b, hgrn_lb,
           ret_decay_logit, head_norm_g, w_branch, w_out, post_ln_g, post_ln_b,
           ffn_w_gate, ffn_w_up, ffn_w_down, moe_w_router, moe_w_gate, moe_w_up, moe_w_down):
    bsz, s_lat, d = x.shape
    s_ctx = ctx.shape[1]
    depth = w_ada.shape[0]
    bw = d // 2
    n_gate = 4 * MLSTM_HEADS
    assert bsz < MOD_ROWS - 1 and s_lat % SCAN_CHUNK == 0 and s_ctx % SCAN_CHUNK == 0
    assert s_lat % GRID_W == 0 and (s_lat <= 1024 or s_lat % 1024 == 0)
    alpha = float((2 * depth) ** 0.25)
    ctx_row = MOD_ROWS // 2
    lat_rows = lambda b: b
    ctx_rows = lambda b: b * 0 + ctx_row

    col = {"mqk": 0, "mv": 2 * bw, "mz": 3 * bw, "hq": 4 * bw, "hi": 5 * bw, "hf": 6 * bw,
           "hg": 8 * bw, "rq": 9 * bw, "rk": 9 * bw + bw // 2, "rv": 10 * bw, "rg": 11 * bw,
           "merge": 12 * bw}
    g0 = 4 * bw

    cv = jnp.zeros((MOD_ROWS, d), F32).at[:bsz].set(c).at[ctx_row].set(c_ctx)
    mod = _ada(cv, w_ada, b_ada)

    h = ctx
    for l in range(depth):
        need_ctx = l < depth - 1
        mod6 = mod[l].reshape(MOD_ROWS * 6, 1, d)
        w_l = w_in[l]
        w_main = jnp.concatenate([w_l[:, :g0], w_l[:, g0 + n_gate:]], axis=1).astype(BF16)
        w_gate = jnp.pad(w_l[:, g0:g0 + n_gate], ((0, 0), (0, LANES - n_gate))).astype(BF16)

        y_x, g_x = _inproj(x, mod6, lat_rows, w_main, w_gate)
        y_h, g_h = _inproj(h, mod6, ctx_rows, w_main, w_gate)
        k_scale = float(bw // MLSTM_HEADS) ** -0.5
        qk = (_conv_silu(y_x, conv_w[l], conv_b[l], True, k_scale),
              _conv_silu(y_h, conv_w[l], conv_b[l], False, k_scale))
        y = (y_x, y_h)
        br_m = _mlstm(qk, y, (g_x, g_h), mlstm_gate_b[l], head_norm_g[l, 0], col["mv"], col["mz"])
        br_h = _hgrn(y, hgrn_lb, l, head_norm_g[l, 1], col["hq"], col["hi"], col["hf"], col["hg"], bw)
        br_r = _retention(y, ret_decay_logit[l], head_norm_g[l, 2],
                          col["rq"], col["rk"], col["rv"], col["rg"],
                          bw // RET_HEADS // 2, bw // RET_HEADS)
        w_br = w_branch[l].astype(BF16)
        w_o = w_out[l].astype(BF16)
        mixed = _merge_proj(br_m[0], br_h[0], br_r[0], y_x, w_br, col["merge"])
        x = _down(mixed, w_o, x, mod6, lat_rows, 2, post_ln_g[l, 0], post_ln_b[l, 0], alpha)
        if need_ctx:
            mixed_h = _merge_proj(br_m[1], br_h[1], br_r[1], y_h, w_br, col["merge"])
            h = _down(mixed_h, w_o, h, mod6, ctx_rows, 2, post_ln_g[l, 0], post_ln_b[l, 0], alpha)

        if l % 2 == 0:
            wg = ffn_w_gate[l // 2].astype(BF16)
            wu = ffn_w_up[l // 2].astype(BF16)
            wd = ffn_w_down[l // 2].astype(BF16)
            x = _down(_ffn_up(x, mod6, lat_rows, wg, wu), wd, x, mod6, lat_rows, 5,
                      post_ln_g[l, 1], post_ln_b[l, 1], alpha)
            if need_ctx:
                h = _down(_ffn_up(h, mod6, ctx_rows, wg, wu), wd, h, mod6, ctx_rows, 5,
                          post_ln_g[l, 1], post_ln_b[l, 1], alpha)
        else:
            e = l // 2
            wg = moe_w_gate[e].astype(BF16)
            wu = moe_w_up[e].astype(BF16)
            wd = moe_w_down[e].astype(BF16)
            x = _moe_layer(x, mod6, lat_rows, moe_w_router[e], wg, wu, wd,
                           post_ln_g[l, 1], post_ln_b[l, 1], alpha)
            if need_ctx:
                h = _moe_layer(h, mod6, ctx_rows, moe_w_router[e], wg, wu, wd,
                               post_ln_g[l, 1], post_ln_b[l, 1], alpha)
    return x
```

```python
import functools

import numpy as np
import jax
import jax.numpy as jnp
from jax import lax
from jax.experimental import pallas as pl
from jax.experimental.pallas import tpu as pltpu

F32 = jnp.float32
BF16 = jnp.bfloat16
I32 = jnp.int32

EPS = 1e-6
NEG = -1e30
TINY = 1e-30

LANES = 128
SUBLANES = 8
SCAN_CHUNK = 256
VMEM_LIMIT_BYTES = 56 * 1024 * 1024
GRID_W = 64
MLSTM_HEADS = 4
HGRN_HEADS = 8
HGRN_HEADS_PER_STEP = 4
RET_HEADS = 4
N_EXPERTS = 8
MOD_ROWS = 16


def _cparams(sem):
    return pltpu.CompilerParams(dimension_semantics=sem, vmem_limit_bytes=VMEM_LIMIT_BYTES)


def _dot(a, b):
    return jnp.dot(a, b, preferred_element_type=F32)


def _dot_nt(a, b):
    return lax.dot_general(a, b, (((1,), (1,)), ((), ())), preferred_element_type=F32)


def _dot_tn(a, b):
    return lax.dot_general(a, b, (((0,), (0,)), ((), ())), preferred_element_type=F32)


def _split_bf16(a):
    hi = a.astype(BF16)
    lo = (a - hi.astype(F32)).astype(BF16)
    return hi, lo


def _silu(a):
    return a * jax.nn.sigmoid(a)


def _ln_rows(a):
    mu = jnp.mean(a, axis=-1, keepdims=True)
    ac = a - mu
    var = jnp.mean(ac * ac, axis=-1, keepdims=True)
    return ac * lax.rsqrt(var + EPS)


def _ada_kernel(cv_ref, w_ref, b_ref, o_ref):
    a = _silu(cv_ref[...])
    a_hi, a_lo = _split_bf16(a)
    w_hi, w_lo = _split_bf16(w_ref[0])
    acc = _dot(a_hi, w_hi) + _dot(a_lo, w_hi) + _dot(a_hi, w_lo)
    o_ref[0] = acc + b_ref[0]


def _ada(cv, w_ada, b_ada):
    n_layers, d, n = w_ada.shape
    tn = 1024
    return pl.pallas_call(
        _ada_kernel,
        out_shape=jax.ShapeDtypeStruct((n_layers, MOD_ROWS, n), F32),
        grid=(n_layers, n // tn),
        in_specs=[pl.BlockSpec((MOD_ROWS, d), lambda l, j: (0, 0)),
                  pl.BlockSpec((1, d, tn), lambda l, j: (l, 0, j)),
                  pl.BlockSpec((1, 1, tn), lambda l, j: (l, 0, j))],
        out_specs=pl.BlockSpec((1, MOD_ROWS, tn), lambda l, j: (l, 0, j)),
        compiler_params=_cparams(("parallel", "parallel")),
        name="ada",
    )(cv, w_ada, b_ada.reshape(n_layers, 1, n))


def _inproj_kernel(x_ref, sh_ref, sc_ref, w_ref, wg_ref, y_ref, g_ref, u_ref):
    j = pl.program_id(2)

    @pl.when(j == 0)
    def _():
        u = _ln_rows(x_ref[0]) * (1.0 + sc_ref[0]) + sh_ref[0]
        ub = u.astype(BF16)
        u_ref[...] = ub
        g_ref[0] = _dot(ub, wg_ref[...])

    y_ref[0] = _dot(u_ref[...], w_ref[...])


def _inproj(xs, mod6, mod_row, w_main, w_gate):
    bsz, s, d = xs.shape
    n = w_main.shape[1]
    tm = min(1024, s)
    tn = 1024
    row = mod_row
    return pl.pallas_call(
        _inproj_kernel,
        out_shape=(jax.ShapeDtypeStruct((bsz, s, n), F32),
                   jax.ShapeDtypeStruct((bsz, s, LANES), F32)),
        grid=(bsz, s // tm, n // tn),
        in_specs=[pl.BlockSpec((1, tm, d), lambda b, i, j: (b, i, 0)),
                  pl.BlockSpec((1, 1, d), lambda b, i, j: (row(b) * 6 + 0, 0, 0)),
                  pl.BlockSpec((1, 1, d), lambda b, i, j: (row(b) * 6 + 1, 0, 0)),
                  pl.BlockSpec((d, tn), lambda b, i, j: (0, j)),
                  pl.BlockSpec((d, LANES), lambda b, i, j: (0, 0))],
        out_specs=(pl.BlockSpec((1, tm, tn), lambda b, i, j: (b, i, j)),
                   pl.BlockSpec((1, tm, LANES), lambda b, i, j: (b, i, 0))),
        scratch_shapes=[pltpu.VMEM((tm, d), BF16)],
        compiler_params=_cparams(("parallel", "parallel", "arbitrary")),
        name="inproj",
    )(xs, mod6, mod6, w_main, w_gate)


_CONV_PAD = 72


def _conv_kernel(a_ref, w_ref, b_ref, o_ref, pad_ref, *, seq, on_grid, k_scale, k_from):
    ct = a_ref.shape[2]
    zeros = jnp.zeros((_CONV_PAD, ct), F32)
    pad_ref[pl.ds(0, _CONV_PAD), :] = zeros
    pad_ref[pl.ds(_CONV_PAD + seq, _CONV_PAD), :] = zeros
    pad_ref[pl.ds(_CONV_PAD, seq), :] = a_ref[0]
    w = w_ref[...]
    bias = b_ref[...]
    scale = jnp.where(pl.program_id(1) >= k_from, k_scale, 1.0).astype(F32)

    rc = min(512, seq)
    for r0 in range(0, seq, rc):
        col = lax.broadcasted_iota(I32, (rc, 1), 0) % GRID_W
        acc = jnp.zeros((rc, ct), F32)
        for dr in range(3) if on_grid else (1,):
            for dc in range(3):
                off = (dr - 1) * GRID_W + (dc - 1)
                tap = pad_ref[pl.ds(_CONV_PAD + r0 + off, rc), :]
                if on_grid and dc == 0:
                    tap = jnp.where(col == 0, 0.0, tap)
                elif on_grid and dc == 2:
                    tap = jnp.where(col == GRID_W - 1, 0.0, tap)
                acc = acc + tap * w[dr * 3 + dc:dr * 3 + dc + 1, :]
        o_ref[0, pl.ds(r0, rc), :] = (_silu(acc + bias) * scale).astype(o_ref.dtype)


def _conv_silu(y, conv_w, conv_b, on_grid, k_scale):
    bsz, seq, _ = y.shape
    ch = conv_w.shape[-1]
    ct = 256
    kern = functools.partial(_conv_kernel, seq=seq, on_grid=on_grid, k_scale=k_scale,
                             k_from=(ch // 2) // ct)
    return pl.pallas_call(
        kern,
        out_shape=jax.ShapeDtypeStruct((bsz, seq, ch), BF16),
        grid=(bsz, ch // ct),
        in_specs=[pl.BlockSpec((1, seq, ct), lambda b, j: (b, 0, j)),
                  pl.BlockSpec((9, ct), lambda b, j: (0, j)),
                  pl.BlockSpec((1, ct), lambda b, j: (0, j))],
        out_specs=pl.BlockSpec((1, seq, ct), lambda b, j: (b, 0, j)),
        scratch_shapes=[pltpu.VMEM((seq + 2 * _CONV_PAD, ct), F32)],
        compiler_params=_cparams(("parallel", "parallel")),
        name="conv_silu",
    )(y, conv_w.reshape(9, ch), conv_b.reshape(1, ch))


def _sched(s, n_lat, n_ctx):
    nc = n_lat + n_ctx
    d = s // nc
    p = s - d * nc
    is_ctx = p < n_ctx
    pc = jnp.minimum(p, n_ctx - 1)
    pq = jnp.maximum(p - n_ctx, 0)
    ctx_c = jnp.where(d == 0, pc, n_ctx - 1 - pc)
    lat_c = jnp.where(d == 0, pq, n_lat - 1 - pq)
    return d, p, is_ctx, lat_c, ctx_c


def _scan_specs(n_lat, n_ctx):
    c = SCAN_CHUNK
    nc = n_lat + n_ctx

    def in_pair(width, col_blk):
        return (pl.BlockSpec((1, c, width), lambda b, h, s: (b, _sched(s, n_lat, n_ctx)[3], col_blk(h, s))),
                pl.BlockSpec((1, c, width), lambda b, h, s: (b, _sched(s, n_lat, n_ctx)[4], col_blk(h, s))))

    def out_pair(width, col_blk):
        return (pl.BlockSpec((1, c, width),
                             lambda b, h, s: (b, _sched(jnp.maximum(s, nc), n_lat, n_ctx)[3], col_blk(h, s))),
                pl.BlockSpec((1, c, width),
                             lambda b, h, s: (b, _sched(jnp.maximum(s, nc), n_lat, n_ctx)[4], col_blk(h, s))))

    return in_pair, out_pair


def _store_scan_out(is_ctx, val, ol_ref, oc_ref):
    @pl.when(is_ctx)
    def _():
        oc_ref[0] = val.astype(oc_ref.dtype)

    @pl.when(jnp.logical_not(is_ctx))
    def _():
        ol_ref[0] = val.astype(ol_ref.dtype)


def _log_sigmoid(a):
    return jnp.minimum(a, 0.0) - jnp.log1p(jnp.exp(-jnp.abs(a)))


def _prefix_sums(cols, rows, tri_b, nh):
    lane = lax.broadcasted_iota(I32, (1, LANES), 1)
    hi = cols.astype(BF16).astype(F32)
    rhs = jnp.where(lane < nh, hi, pltpu.roll(cols - hi, nh, axis=1)).astype(BF16)
    bc = _dot(tri_b, rhs)
    col = bc + pltpu.roll(bc, LANES - nh, axis=1)
    sub = lax.broadcasted_iota(I32, (SUBLANES, 1), 0)
    rhi = rows.astype(BF16).astype(F32)
    lhs = jnp.where(sub < nh, rhi, pltpu.roll(rows - rhi, nh, axis=0))
    lhs = jnp.concatenate([lhs, jnp.zeros_like(lhs)], axis=0).astype(BF16)
    br = _dot_nt(lhs, tri_b)[0:SUBLANES]
    row = br + pltpu.roll(br, SUBLANES - nh, axis=0)
    return col, row


def _mlstm_kernel(ql_ref, qc_ref, kl_ref, kc_ref, vl_ref, vc_ref, zl_ref, zc_ref, gl_ref, gc_ref,
                  gb_ref, ng_ref, ol_ref, oc_ref, c_ref, n_ref, m_ref, hbuf_ref, *, n_lat, n_ctx):
    nh = MLSTM_HEADS
    d, p, is_ctx, lat_c, ctx_c = _sched(pl.program_id(2), n_lat, n_ctx)
    chunk = jnp.where(is_ctx, n_lat + ctx_c, lat_c)
    pick = lambda a_l, a_c: jnp.where(is_ctx, a_c[0], a_l[0])
    c = SCAN_CHUNK
    dh = c_ref.shape[1]

    @pl.when(p == 0)
    def _():
        c_ref[...] = jnp.zeros_like(c_ref)
        n_ref[...] = jnp.zeros_like(n_ref)
        m_ref[...] = jnp.full_like(m_ref, NEG)

    gates = pick(gl_ref, gc_ref) + gb_ref[...]
    gates = jnp.where(d == 0, gates, pltpu.roll(gates, LANES - 2 * nh, axis=1))
    lane = lax.broadcasted_iota(I32, (1, LANES), 1)
    r32 = jnp.where(lane < nh, gates, jnp.where(lane < 2 * nh, _log_sigmoid(gates), 0.0))
    rt = r32.T[0:SUBLANES]
    lf_cols = jnp.where(lane < nh, pltpu.roll(r32, LANES - nh, axis=1), 0.0)
    sub = lax.broadcasted_iota(I32, (SUBLANES, 1), 0)
    lf_rows = jnp.where(sub < nh, pltpu.roll(rt, SUBLANES - nh, axis=0), 0.0)

    t_i = lax.broadcasted_iota(I32, (c, c), 0)
    s_i = lax.broadcasted_iota(I32, (c, c), 1)
    tri = (t_i - s_i) * (1 - 2 * d) >= 0
    b_cols, b_rows = _prefix_sums(lf_cols, lf_rows, tri.astype(BF16), nh)
    totals = jnp.sum(lf_cols, axis=0, keepdims=True)

    q_all = pick(ql_ref, qc_ref)
    k_all = pick(kl_ref, kc_ref)
    v_all = pick(vl_ref, vc_ref)
    outs = []
    for j in range(nh):
        li = r32[:, j:j + 1]
        li_row = rt[j:j + 1, :]
        b_col = b_cols[:, j:j + 1]
        b_row = b_rows[j:j + 1, :]
        total = totals[:, j:j + 1]
        q = q_all[:, j * dh:(j + 1) * dh]
        k = k_all[:, j * dh:(j + 1) * dh]
        vb = v_all[:, j * dh:(j + 1) * dh].astype(BF16)

        m_prev = m_ref[j, 0:1, 0:1]
        d_mat = jnp.where(tri, b_col - b_row + li_row, NEG)
        a_inter = b_col + m_prev
        m_t = jnp.maximum(a_inter, jnp.max(d_mat, axis=1, keepdims=True))
        w_inter = jnp.exp(a_inter - m_t)
        s_mat = _dot_nt(q, k) * jnp.exp(d_mat - m_t)
        num = w_inter * _dot(q, c_ref[j].astype(BF16)) + _dot(s_mat.astype(BF16), vb)
        den = (w_inter * jnp.sum(q.astype(F32) * n_ref[j, 0:1, :], axis=1, keepdims=True)
               + jnp.sum(s_mat, axis=1, keepdims=True))
        outs.append(num / jnp.maximum(jnp.abs(den), jnp.exp(-m_t)))

        g_col = total - b_col + li
        m_new = jnp.maximum(total + m_prev, jnp.max(g_col, axis=0, keepdims=True))
        decay = jnp.exp(total + m_prev - m_new)
        kw = k.astype(F32) * jnp.exp(g_col - m_new)
        c_ref[j] = decay * c_ref[j] + _dot_tn(kw.astype(BF16), vb)
        n_new = decay * n_ref[j, 0:1, :] + jnp.sum(kw, axis=0, keepdims=True)
        n_ref[j] = jnp.broadcast_to(n_new, n_ref.shape[1:])
        m_ref[j] = jnp.broadcast_to(m_new, m_ref.shape[1:])

    @pl.when(d == 0)
    def _():
        for j in range(nh):
            hbuf_ref[chunk, :, j * dh:(j + 1) * dh] = outs[j]

    @pl.when(d == 1)
    def _():
        z = pick(zl_ref, zc_ref)
        fin = []
        for j in range(nh):
            sl = slice(j * dh, (j + 1) * dh)
            hn = _ln_rows(hbuf_ref[chunk, :, sl] + outs[j])
            fin.append(hn * ng_ref[:, sl] * _silu(z[:, sl]))
        _store_scan_out(is_ctx, jnp.concatenate(fin, axis=1), ol_ref, oc_ref)


def _mlstm(qk, y, g, gate_b, norm_g, v_col, z_col):
    bsz, s_lat, w2 = qk[0].shape
    s_ctx = qk[1].shape[1]
    width = w2 // 2
    dh = width // MLSTM_HEADS
    c = SCAN_CHUNK
    n_lat, n_ctx = s_lat // c, s_ctx // c
    in_pair, out_pair = _scan_specs(n_lat, n_ctx)
    kern = functools.partial(_mlstm_kernel, n_lat=n_lat, n_ctx=n_ctx)
    gb = jnp.zeros((1, LANES), F32).at[0, :gate_b.shape[0]].set(gate_b)
    return pl.pallas_call(
        kern,
        out_shape=(jax.ShapeDtypeStruct((bsz, s_lat, width), BF16),
                   jax.ShapeDtypeStruct((bsz, s_ctx, width), BF16)),
        grid=(bsz, 1, 2 * (n_lat + n_ctx)),
        in_specs=[*in_pair(width, lambda h, s: 0),
                  *in_pair(width, lambda h, s: 1),
                  *in_pair(width, lambda h, s: v_col // width),
                  *out_pair(width, lambda h, s: z_col // width),
                  *in_pair(LANES, lambda h, s: 0),
                  pl.BlockSpec((1, LANES), lambda b, h, s: (0, 0)),
                  pl.BlockSpec((1, width), lambda b, h, s: (0, 0))],
        out_specs=out_pair(width, lambda h, s: 0),
        scratch_shapes=[pltpu.VMEM((MLSTM_HEADS, dh, dh), F32),
                        pltpu.VMEM((MLSTM_HEADS, SUBLANES, dh), F32),
                        pltpu.VMEM((MLSTM_HEADS, SUBLANES, LANES), F32),
                        pltpu.VMEM((n_lat + n_ctx, c, width), F32)],
        compiler_params=_cparams(("parallel", "parallel", "arbitrary")),
        name="mlstm",
    )(qk[0], qk[1], qk[0], qk[1], y[0], y[1], y[0], y[1], g[0], g[1], gb, norm_g.reshape(1, width))


def _ret_kernel(ql_ref, qc_ref, kl_ref, kc_ref, vl_ref, vc_ref, zl_ref, zc_ref, dl_ref, ng_ref,
                ol_ref, oc_ref, s_ref, dmat_ref, hbuf_ref, *, n_lat, n_ctx, q_scale):
    nh = RET_HEADS
    d, p, is_ctx, lat_c, ctx_c = _sched(pl.program_id(2), n_lat, n_ctx)
    chunk = jnp.where(is_ctx, n_lat + ctx_c, lat_c)
    pick = lambda a_l, a_c: jnp.where(is_ctx, a_c[0], a_l[0])
    c = SCAN_CHUNK
    dk, dv = s_ref.shape[1], s_ref.shape[2]

    lane = lax.broadcasted_iota(I32, (1, LANES), 1)
    lgs = _log_sigmoid(dl_ref[...])
    lg = [jnp.sum(jnp.where(lane == d * nh + j, lgs, 0.0), axis=1, keepdims=True) for j in range(nh)]

    @pl.when(p == 0)
    def _():
        s_ref[...] = jnp.zeros_like(s_ref)
        t_i = lax.broadcasted_iota(I32, (c, c), 0)
        s_i = lax.broadcasted_iota(I32, (c, c), 1)
        dist = (t_i - s_i) * (1 - 2 * d)
        distf = jnp.maximum(dist, 0).astype(F32)
        for j in range(nh):
            dmat_ref[j] = jnp.where(dist >= 0, jnp.exp(lg[j] * distf), 0.0)

    t_c = lax.broadcasted_iota(I32, (c, 1), 0)
    pos = jnp.where(d == 0, t_c + 1, c - t_c).astype(F32)
    q_all = pick(ql_ref, qc_ref)
    k_all = pick(kl_ref, kc_ref)
    v_all = pick(vl_ref, vc_ref)
    outs = []
    for j in range(nh):
        qb = (q_all[:, j * dk:(j + 1) * dk] * q_scale).astype(BF16)
        kf = k_all[:, j * dk:(j + 1) * dk]
        vb = v_all[:, j * dv:(j + 1) * dv].astype(BF16)
        a = _dot_nt(qb, kf.astype(BF16)) * dmat_ref[j]
        outs.append(jnp.exp(lg[j] * pos) * _dot(qb, s_ref[j].astype(BF16)) + _dot(a.astype(BF16), vb))
        kw = kf * jnp.exp(lg[j] * (c - pos))
        s_ref[j] = jnp.exp(lg[j] * c) * s_ref[j] + _dot_tn(kw.astype(BF16), vb)

    @pl.when(d == 0)
    def _():
        for j in range(nh):
            hbuf_ref[chunk, :, j * dv:(j + 1) * dv] = outs[j]

    @pl.when(d == 1)
    def _():
        z = pick(zl_ref, zc_ref)
        fin = []
        for j in range(nh):
            sl = slice(j * dv, (j + 1) * dv)
            hn = _ln_rows(hbuf_ref[chunk, :, sl] + outs[j])
            fin.append(hn * ng_ref[:, sl] * _silu(z[:, sl]))
        _store_scan_out(is_ctx, jnp.concatenate(fin, axis=1), ol_ref, oc_ref)


def _retention(y, decay_logit, norm_g, q_col, k_col, v_col, z_col, dk, dv):
    bsz, s_lat, _ = y[0].shape
    s_ctx = y[1].shape[1]
    c = SCAN_CHUNK
    n_lat, n_ctx = s_lat // c, s_ctx // c
    in_pair, out_pair = _scan_specs(n_lat, n_ctx)
    kern = functools.partial(_ret_kernel, n_lat=n_lat, n_ctx=n_ctx, q_scale=float(dk) ** -0.5)
    wk, wv = RET_HEADS * dk, RET_HEADS * dv
    dl = jnp.zeros((1, LANES), F32).at[0, :2 * RET_HEADS].set(decay_logit.reshape(-1))
    return pl.pallas_call(
        kern,
        out_shape=(jax.ShapeDtypeStruct((bsz, s_lat, RET_HEADS * dv), BF16),
                   jax.ShapeDtypeStruct((bsz, s_ctx, RET_HEADS * dv), BF16)),
        grid=(bsz, 1, 2 * (n_lat + n_ctx)),
        in_specs=[*in_pair(wk, lambda h, s: q_col // wk),
                  *in_pair(wk, lambda h, s: k_col // wk),
                  *in_pair(wv, lambda h, s: v_col // wv),
                  *out_pair(wv, lambda h, s: z_col // wv),
                  pl.BlockSpec((1, LANES), lambda b, h, s: (0, 0)),
                  pl.BlockSpec((1, wv), lambda b, h, s: (0, 0))],
        out_specs=out_pair(wv, lambda h, s: 0),
        scratch_shapes=[pltpu.VMEM((RET_HEADS, dk, dv), F32),
                        pltpu.VMEM((RET_HEADS, c, c), F32),
                        pltpu.VMEM((n_lat + n_ctx, c, wv), F32)],
        compiler_params=_cparams(("parallel", "parallel", "arbitrary")),
        name="retention",
    )(y[0], y[1], y[0], y[1], y[0], y[1], y[0], y[1], dl, norm_g.reshape(1, RET_HEADS * dv))


def _hgrn_tables(c):
    nl = int(np.log2(c))
    t = np.arange(c)[:, None]
    u = np.arange(c)[None, :]
    x = (t ^ u).astype(np.int64)
    lvl = np.where(x == 0, nl, np.floor(np.log2(np.maximum(x, 1))).astype(np.int64))
    return (jnp.asarray((u <= t).astype(np.float32), BF16), jnp.asarray(lvl, I32), nl)


def _block_ref_rows(b, lev):
    c, w = b.shape
    n = 1 << lev
    if 2 * n >= SUBLANES:
        b3 = b.reshape(c // (2 * n), 2 * n, w)
        return jnp.broadcast_to(b3[:, n - 1:n, :], b3.shape).reshape(c, w)
    t = lax.broadcasted_iota(I32, (c, 1), 0) & (2 * n - 1)
    out = b
    for k in range(2 * n):
        if k != n - 1:
            out = jnp.where(t == k, pltpu.roll(b, (k - (n - 1)) % c, axis=0), out)
    return out


def _hgrn_kernel(ql_ref, qc_ref, vl_ref, vc_ref, fl_ref, fc_ref, zl_ref, zc_ref, lb_ref, ng_ref,
                 tab_ref, lvl_ref, ol_ref, oc_ref, st_ref, hbuf_ref, *, n_lat, n_ctx, layer, nl):
    d, p, is_ctx, lat_c, ctx_c = _sched(pl.program_id(2), n_lat, n_ctx)
    chunk = jnp.where(is_ctx, n_lat + ctx_c, lat_c)
    pick = lambda a_l, a_c: jnp.where(is_ctx, a_c[0], a_l[0])
    c = SCAN_CHUNK

    @pl.when(p == 0)
    def _():
        st_ref[...] = jnp.zeros_like(st_ref)

    lbp = lb_ref[0]
    sm = jnp.exp(lbp - jnp.max(lbp, axis=0, keepdims=True))
    sm = sm / jnp.sum(sm, axis=0, keepdims=True)
    lb = jnp.zeros((1, lbp.shape[1]), F32)
    for i in range(1, layer + 1):
        lb = lb + sm[i:i + 1, :]

    ft = pick(fl_ref, fc_ref)
    kk = (1.0 - lb) * jax.nn.sigmoid(-ft)
    lf = jnp.log(jnp.maximum(lb + (1.0 - lb) * jax.nn.sigmoid(ft), TINY))
    q = _silu(pick(ql_ref, qc_ref))
    v_all = pick(vl_ref, vc_ref)
    w = lf.shape[1]
    dk = st_ref.shape[1]
    nh = w // dk
    lf_hi = lf.astype(BF16)
    r1 = lf - lf_hi.astype(F32)
    lf_mid = r1.astype(BF16)
    lf_lo = (r1 - lf_mid.astype(F32)).astype(BF16)
    ps = _dot(tab_ref[...], jnp.concatenate([lf_hi, lf_mid, lf_lo], axis=1))
    b = ps[:, :w] + ps[:, w:2 * w] + ps[:, 2 * w:]
    total = jnp.sum(lf, axis=0, keepdims=True)
    bx = b - jnp.where(d == 0, 0.0, 1.0) * lf

    def seg_exp(idx):
        if idx < nl:
            return jnp.exp(-jnp.abs(bx - _block_ref_rows(b, idx)))
        query_side = (idx == nl)
        from_start = jnp.where(d == 0, 1.0, 0.0) if query_side else jnp.where(d == 0, 0.0, 1.0)
        return jnp.exp(from_start * bx + (1.0 - from_start) * (total - bx))

    t_c = lax.broadcasted_iota(I32, (c, 1), 0)
    lvl = lvl_ref[...]
    hs = [slice(j * dk, (j + 1) * dk) for j in range(nh)]
    qb = q.astype(BF16)
    kb = kk.astype(BF16)
    accs = [jnp.where(lvl == nl, _dot_nt(qb[:, sl], kb[:, sl]), 0.0) for sl in hs]
    for lev in range(nl):
        e = seg_exp(lev)
        is_q = ((t_c >> lev) & 1) != d
        qt = jnp.where(is_q, q * e, 0.0).astype(BF16)
        kt = jnp.where(is_q, 0.0, kk * e).astype(BF16)
        hit = lvl == lev
        accs = [jnp.where(hit, _dot_nt(qt[:, sl], kt[:, sl]), a) for sl, a in zip(hs, accs)]

    qg = (q * seg_exp(nl)).astype(BF16)
    kg = (kk * seg_exp(nl + 1)).astype(BF16)
    decay = jnp.exp(jnp.sum(lf, axis=0, keepdims=True))
    outs = []
    for j, sl in enumerate(hs):
        st = st_ref[j]
        vb = v_all[:, sl].astype(BF16)
        outs.append(_dot(accs[j].astype(BF16), vb) + _dot_nt(qg[:, sl], st.astype(BF16)))
        st_ref[j] = st * decay[:, sl] + _dot_tn(vb, kg[:, sl])

    @pl.when(d == 0)
    def _():
        for j, sl in enumerate(hs):
            hbuf_ref[chunk, :, sl] = outs[j]

    @pl.when(d == 1)
    def _():
        z = pick(zl_ref, zc_ref)
        fin = []
        for j, sl in enumerate(hs):
            tot = hbuf_ref[chunk, :, sl] + outs[j]
            hn = tot * lax.rsqrt(jnp.mean(tot * tot, axis=-1, keepdims=True) + EPS)
            fin.append(hn * ng_ref[:, sl] * _silu(z[:, sl]))
        _store_scan_out(is_ctx, jnp.concatenate(fin, axis=1), ol_ref, oc_ref)


def _hgrn(y, hgrn_lb, layer, norm_g, q_col, v_col, f_col, z_col, width):
    bsz, s_lat, _ = y[0].shape
    s_ctx = y[1].shape[1]
    dk = width // HGRN_HEADS
    depth = hgrn_lb.shape[1]
    c = SCAN_CHUNK
    n_lat, n_ctx = s_lat // c, s_ctx // c
    nc = n_lat + n_ctx
    tabs, lvl, nl = _hgrn_tables(c)
    in_pair, out_pair = _scan_specs(n_lat, n_ctx)
    kern = functools.partial(_hgrn_kernel, n_lat=n_lat, n_ctx=n_ctx, layer=layer, nl=nl)
    hp = HGRN_HEADS_PER_STEP
    wg = hp * dk
    per_dir = width // wg
    return pl.pallas_call(
        kern,
        out_shape=(jax.ShapeDtypeStruct((bsz, s_lat, width), BF16),
                   jax.ShapeDtypeStruct((bsz, s_ctx, width), BF16)),
        grid=(bsz, HGRN_HEADS // hp, 2 * nc),
        in_specs=[*in_pair(wg, lambda h, s: q_col // wg + h),
                  *in_pair(wg, lambda h, s: v_col // wg + h),
                  *in_pair(wg, lambda h, s: f_col // wg + (s // nc) * per_dir + h),
                  *out_pair(wg, lambda h, s: z_col // wg + h),
                  pl.BlockSpec((1, depth, wg), lambda b, h, s: (s // nc, 0, h)),
                  pl.BlockSpec((1, wg), lambda b, h, s: (0, h)),
                  pl.BlockSpec((c, c), lambda b, h, s: (0, 0)),
                  pl.BlockSpec((c, c), lambda b, h, s: (0, 0))],
        out_specs=out_pair(wg, lambda h, s: h),
        scratch_shapes=[pltpu.VMEM((hp, dk, dk), F32),
                        pltpu.VMEM((nc, c, wg), F32)],
        compiler_params=_cparams(("parallel", "parallel", "arbitrary")),
        name="hgrn2",
    )(y[0], y[1], y[0], y[1], y[0], y[1], y[0], y[1], hgrn_lb, norm_g.reshape(1, width), tabs, lvl)


def _merge_kernel(bm_ref, bh_ref, br_ref, g0_ref, g1_ref, g2_ref, w_ref, o_ref):
    acc = jax.nn.sigmoid(g0_ref[0]) * _dot(bm_ref[0], w_ref[0])
    acc = acc + jax.nn.sigmoid(g1_ref[0]) * _dot(bh_ref[0], w_ref[1])
    acc = acc + jax.nn.sigmoid(g2_ref[0]) * _dot(br_ref[0], w_ref[2])
    o_ref[0] = acc.astype(o_ref.dtype)


def _merge_proj(br_m, br_h, br_r, y, w_br, merge_col):
    bsz, s, width = br_m.shape
    d = w_br.shape[2]
    tm = min(1024, s)
    tn = 512
    gcol = merge_col // tn
    nj = d // tn
    bspec = pl.BlockSpec((1, tm, width), lambda b, i, j: (b, i, 0))

    def gspec(n):
        return pl.BlockSpec((1, tm, tn), lambda b, i, j: (b, i, gcol + n * nj + j))

    return pl.pallas_call(
        _merge_kernel,
        out_shape=jax.ShapeDtypeStruct((bsz, s, d), BF16),
        grid=(bsz, s // tm, nj),
        in_specs=[bspec, bspec, bspec, gspec(0), gspec(1), gspec(2),
                  pl.BlockSpec((3, width, tn), lambda b, i, j: (0, 0, j))],
        out_specs=pl.BlockSpec((1, tm, tn), lambda b, i, j: (b, i, j)),
        compiler_params=_cparams(("parallel", "parallel", "arbitrary")),
        name="merge_proj",
    )(br_m, br_h, br_r, y, y, y, w_br)


def _down_kernel(a_ref, w_ref, x_ref, gate_ref, g_ref, b_ref, o_ref, acc_ref, *, alpha, nk):
    k = pl.program_id(2)

    @pl.when(k == 0)
    def _():
        acc_ref[...] = jnp.zeros_like(acc_ref)

    acc_ref[...] += _dot(a_ref[0], w_ref[...])

    @pl.when(k == nk - 1)
    def _():
        z = alpha * x_ref[0] + gate_ref[0] * acc_ref[...]
        o_ref[0] = _ln_rows(z) * g_ref[...] + b_ref[...]


def _down(a, w, xs, mod6, mod_row, which, ln_g, ln_b, alpha):
    bsz, s, d = xs.shape
    kdim = w.shape[0]
    tm = min(1024, s)
    tk = 512 if kdim % 512 == 0 else kdim
    nk = kdim // tk
    row = mod_row
    kern = functools.partial(_down_kernel, alpha=alpha, nk=nk)
    return pl.pallas_call(
        kern,
        out_shape=jax.ShapeDtypeStruct((bsz, s, d), F32),
        grid=(bsz, s // tm, nk),
        in_specs=[pl.BlockSpec((1, tm, tk), lambda b, i, k: (b, i, k)),
                  pl.BlockSpec((tk, d), lambda b, i, k: (k, 0)),
                  pl.BlockSpec((1, tm, d), lambda b, i, k: (b, i, 0)),
                  pl.BlockSpec((1, 1, d), lambda b, i, k: (row(b) * 6 + which, 0, 0)),
                  pl.BlockSpec((1, d), lambda b, i, k: (0, 0)),
                  pl.BlockSpec((1, d), lambda b, i, k: (0, 0))],
        out_specs=pl.BlockSpec((1, tm, d), lambda b, i, k: (b, i, 0)),
        scratch_shapes=[pltpu.VMEM((tm, d), F32)],
        compiler_params=_cparams(("parallel", "parallel", "arbitrary")),
        name="down_postnorm",
    )(a, w, xs, mod6, ln_g.reshape(1, d), ln_b.reshape(1, d))


def _ffn_up_kernel(x_ref, sh_ref, sc_ref, wg_ref, wu_ref, h_ref, u_ref):
    @pl.when(pl.program_id(2) == 0)
    def _():
        u = _ln_rows(x_ref[0]) * (1.0 + sc_ref[0]) + sh_ref[0]
        u_ref[...] = u.astype(BF16)

    ub = u_ref[...]
    h_ref[0] = (_silu(_dot(ub, wg_ref[...])) * _dot(ub, wu_ref[...])).astype(h_ref.dtype)


def _ffn_up(xs, mod6, mod_row, w_gate, w_up):
    bsz, s, d = xs.shape
    f = w_gate.shape[1]
    tm = min(1024, s)
    tn = 512
    row = mod_row
    return pl.pallas_call(
        _ffn_up_kernel,
        out_shape=jax.ShapeDtypeStruct((bsz, s, f), BF16),
        grid=(bsz, s // tm, f // tn),
        in_specs=[pl.BlockSpec((1, tm, d), lambda b, i, j: (b, i, 0)),
                  pl.BlockSpec((1, 1, d), lambda b, i, j: (row(b) * 6 + 3, 0, 0)),
                  pl.BlockSpec((1, 1, d), lambda b, i, j: (row(b) * 6 + 4, 0, 0)),
                  pl.BlockSpec((d, tn), lambda b, i, j: (0, j)),
                  pl.BlockSpec((d, tn), lambda b, i, j: (0, j))],
        out_specs=pl.BlockSpec((1, tm, tn), lambda b, i, j: (b, i, j)),
        scratch_shapes=[pltpu.VMEM((tm, d), BF16)],
        compiler_params=_cparams(("parallel", "parallel", "arbitrary")),
        name="ffn_up",
    )(xs, mod6, mod6, w_gate, w_up)


MOE_ROW_TILE = 512


def _router_kernel(x_ref, sh_ref, sc_ref, w_ref, u_ref, rt_ref, tot_ref, carry_ref, *, n_exp):
    i = pl.program_id(0)

    @pl.when(i == 0)
    def _():
        carry_ref[...] = jnp.zeros_like(carry_ref)

    u = _ln_rows(x_ref[...]) * (1.0 + sc_ref[0]) + sh_ref[0]
    u_ref[...] = u
    u_hi, u_lo = _split_bf16(u)
    w_hi, w_lo = _split_bf16(w_ref[...])
    logits = _dot(u_hi, w_hi) + _dot(u_lo, w_hi) + _dot(u_hi, w_lo)
    tr = logits.shape[0]
    lane = lax.broadcasted_iota(I32, (1, LANES), 1)
    ninf = jnp.float32(-jnp.inf)
    lg = jnp.where(lane < n_exp, logits, ninf)
    m1 = jnp.max(lg, axis=1, keepdims=True)
    i1 = jnp.min(jnp.where(lg == m1, lane, LANES), axis=1, keepdims=True)
    lg2 = jnp.where(lane == i1, ninf, lg)
    m2 = jnp.max(lg2, axis=1, keepdims=True)
    i2 = jnp.min(jnp.where(lg2 == m2, lane, LANES), axis=1, keepdims=True)
    e = jnp.exp(m2 - m1)
    w0 = 1.0 / (1.0 + e)
    w1 = e / (1.0 + e)

    oh = ((lane == i1) | (lane == i2)).astype(F32)
    r_i = lax.broadcasted_iota(I32, (tr, tr), 0)
    c_i = lax.broadcasted_iota(I32, (tr, tr), 1)
    cum = _dot((c_i < r_i).astype(BF16), oh.astype(BF16)) + carry_ref[0:1, :]
    r0 = jnp.sum(jnp.where(lane == i1, cum, 0.0), axis=1, keepdims=True)
    r1 = jnp.sum(jnp.where(lane == i2, cum, 0.0), axis=1, keepdims=True)
    new_carry = carry_ref[0:1, :] + jnp.sum(oh, axis=0, keepdims=True)
    carry_ref[...] = jnp.broadcast_to(new_carry, carry_ref.shape)
    tot_ref[...] = jnp.broadcast_to(new_carry, tot_ref.shape)

    rt = jnp.where(lane == 0, i1.astype(F32), 0.0)
    rt = jnp.where(lane == 1, i2.astype(F32), rt)
    rt = jnp.where(lane == 2, w0, rt)
    rt = jnp.where(lane == 3, w1, rt)
    rt = jnp.where(lane == 4, r0, rt)
    rt = jnp.where(lane == 5, r1, rt)
    rt_ref[...] = rt


def _router(x2, mod6, mod_row, seq, w_router):
    t, d = x2.shape
    n_exp = w_router.shape[1]
    tr = min(512, seq)
    wr = jnp.zeros((d, LANES), F32).at[:, :n_exp].set(w_router)
    kern = functools.partial(_router_kernel, n_exp=n_exp)
    per = seq // tr
    return pl.pallas_call(
        kern,
        out_shape=(jax.ShapeDtypeStruct((t, d), F32),
                   jax.ShapeDtypeStruct((t, LANES), F32),
                   jax.ShapeDtypeStruct((SUBLANES, LANES), F32)),
        grid=(t // tr,),
        in_specs=[pl.BlockSpec((tr, d), lambda i: (i, 0)),
                  pl.BlockSpec((1, 1, d), lambda i: (mod_row(i // per) * 6 + 3, 0, 0)),
                  pl.BlockSpec((1, 1, d), lambda i: (mod_row(i // per) * 6 + 4, 0, 0)),
                  pl.BlockSpec((d, LANES), lambda i: (0, 0))],
        out_specs=(pl.BlockSpec((tr, d), lambda i: (i, 0)),
                   pl.BlockSpec((tr, LANES), lambda i: (i, 0)),
                   pl.BlockSpec((SUBLANES, LANES), lambda i: (0, 0))),
        scratch_shapes=[pltpu.VMEM((SUBLANES, LANES), F32)],
        compiler_params=_cparams(("arbitrary",)),
        name="moe_router",
    )(x2, mod6, mod6, wr)


def _dispatch_kernel(tot_ref, e0_ref, e1_ref, r0_ref, r1_ref, u_ref,
                     xs_hbm, d0_ref, d1_ref, be_ref, gs_ref, zero_ref, sem, zsem,
                     *, n_exp, n_blocks, tile, td):
    i = pl.program_id(0)

    @pl.when(i == 0)
    def _():
        start = jnp.int32(0)
        for e in range(n_exp):
            gs_ref[e] = start
            start = start + ((tot_ref[e] + (tile - 1)) // tile) * tile
        gs_ref[n_exp] = start

        def blk(b, carry):
            row = b * tile
            ex = jnp.int32(0)
            for e in range(1, n_exp):
                ex = jnp.where(row >= gs_ref[e], e, ex)
            be_ref[0, b] = ex
            be_ref[1, b] = jnp.where(row < gs_ref[n_exp], 1, 0).astype(I32)
            return carry

        lax.fori_loop(0, n_blocks, blk, 0)

        zero_ref[...] = jnp.zeros_like(zero_ref)

        def zstart(b, carry):
            pltpu.make_async_copy(zero_ref, xs_hbm.at[pl.ds(b * tile, tile)], zsem).start()
            return carry

        def zwait(b, carry):
            pltpu.make_async_copy(zero_ref, xs_hbm.at[pl.ds(0, tile)], zsem).wait()
            return carry

        lax.fori_loop(0, n_blocks, zstart, 0)
        lax.fori_loop(0, n_blocks, zwait, 0)

    def row(t, carry):
        da = gs_ref[e0_ref[0, 0, t]] + r0_ref[0, 0, t]
        db = gs_ref[e1_ref[0, 0, t]] + r1_ref[0, 0, t]
        d0_ref[0, 0, t] = da
        d1_ref[0, 0, t] = db
        pltpu.make_async_copy(u_ref.at[pl.ds(t, 1)], xs_hbm.at[pl.ds(da, 1)], sem).start()
        pltpu.make_async_copy(u_ref.at[pl.ds(t, 1)], xs_hbm.at[pl.ds(db, 1)], sem).start()
        return carry

    def row_wait(t, carry):
        pltpu.make_async_copy(u_ref.at[pl.ds(0, 1)], xs_hbm.at[pl.ds(0, 1)], sem).wait()
        return carry

    lax.fori_loop(0, td, row, 0)
    lax.fori_loop(0, 2 * td, row_wait, 0)


def _dispatch(tot_i, e0, e1, r0, r1, u2, n_blocks, tile):
    t, d = u2.shape
    td = min(512, t)
    n_exp = N_EXPERTS
    nt = t // td
    resh = lambda a: a.reshape(nt, 1, td)
    smem_blk = pl.BlockSpec((1, 1, td), lambda i, tot: (i, 0, 0), memory_space=pltpu.SMEM)
    kern = functools.partial(_dispatch_kernel, n_exp=n_exp, n_blocks=n_blocks, tile=tile, td=td)
    grid_spec = pltpu.PrefetchScalarGridSpec(
        num_scalar_prefetch=1,
        grid=(nt,),
        in_specs=[smem_blk, smem_blk, smem_blk, smem_blk,
                  pl.BlockSpec((td, d), lambda i, tot: (i, 0))],
        out_specs=(pl.BlockSpec(memory_space=pl.ANY), smem_blk, smem_blk,
                   pl.BlockSpec((2, n_blocks), lambda i, tot: (0, 0), memory_space=pltpu.SMEM)),
        scratch_shapes=[pltpu.SMEM((n_exp + 1,), I32),
                        pltpu.VMEM((tile, d), F32),
                        pltpu.SemaphoreType.DMA(()),
                        pltpu.SemaphoreType.DMA(())],
    )
    xs, d0, d1, be = pl.pallas_call(
        kern,
        out_shape=(jax.ShapeDtypeStruct((n_blocks * tile, d), F32),
                   jax.ShapeDtypeStruct((nt, 1, td), I32),
                   jax.ShapeDtypeStruct((nt, 1, td), I32),
                   jax.ShapeDtypeStruct((2, n_blocks), I32)),
        grid_spec=grid_spec,
        compiler_params=_cparams(("arbitrary",)),
        name="moe_dispatch",
    )(tot_i, resh(e0), resh(e1), resh(r0), resh(r1), u2)
    return xs, d0.reshape(t), d1.reshape(t), be


def _expert_kernel(be_ref, x_ref, wg_ref, wu_ref, wd_ref, y_ref, xb_ref, acc_ref, *, nf):
    b = pl.program_id(0)
    f = pl.program_id(1)
    valid = be_ref[1, b] == 1

    @pl.when(valid)
    def _():
        @pl.when(f == 0)
        def _():
            xb_ref[...] = x_ref[...].astype(BF16)
            acc_ref[...] = jnp.zeros_like(acc_ref)

        xb = xb_ref[...]
        hmid = _silu(_dot(xb, wg_ref[0])) * _dot(xb, wu_ref[0])
        acc_ref[...] += _dot(hmid.astype(BF16), wd_ref[0])

        @pl.when(f == nf - 1)
        def _():
            y_ref[...] = acc_ref[...]

    @pl.when(jnp.logical_not(valid) & (f == 0))
    def _():
        y_ref[...] = jnp.zeros_like(y_ref)


def _expert_ffn(be, xs, w_gate, w_up, w_down, tile):
    rows, d = xs.shape
    n_blocks = rows // tile
    fdim = w_gate.shape[2]
    tf = 512
    nf = fdim // tf

    def fsel(b, f, be_ref):
        return jnp.where(be_ref[1, b] == 1, f, nf - 1)

    grid_spec = pltpu.PrefetchScalarGridSpec(
        num_scalar_prefetch=1,
        grid=(n_blocks, nf),
        in_specs=[pl.BlockSpec((tile, d), lambda b, f, be_ref: (b, 0)),
                  pl.BlockSpec((1, d, tf), lambda b, f, be_ref: (be_ref[0, b], 0, fsel(b, f, be_ref))),
                  pl.BlockSpec((1, d, tf), lambda b, f, be_ref: (be_ref[0, b], 0, fsel(b, f, be_ref))),
                  pl.BlockSpec((1, tf, d), lambda b, f, be_ref: (be_ref[0, b], fsel(b, f, be_ref), 0))],
        out_specs=pl.BlockSpec((tile, d), lambda b, f, be_ref: (b, 0)),
        scratch_shapes=[pltpu.VMEM((tile, d), BF16), pltpu.VMEM((tile, d), F32)],
    )
    return pl.pallas_call(
        functools.partial(_expert_kernel, nf=nf),
        out_shape=jax.ShapeDtypeStruct((rows, d), F32),
        grid_spec=grid_spec,
        compiler_params=_cparams(("parallel", "arbitrary")),
        name="moe_expert_ffn",
    )(be, xs, w_gate, w_up, w_down)


def _combine_kernel(d0_ref, d1_ref, rt_ref, x_ref, gate_ref, g_ref, b_ref, ys_hbm, o_ref,
                    buf_ref, sem, *, alpha, tc):
    def start(t, carry):
        pltpu.make_async_copy(ys_hbm.at[pl.ds(d0_ref[0, 0, t], 1)], buf_ref.at[0, pl.ds(t, 1)], sem).start()
        pltpu.make_async_copy(ys_hbm.at[pl.ds(d1_ref[0, 0, t], 1)], buf_ref.at[1, pl.ds(t, 1)], sem).start()
        return carry

    def wait(t, carry):
        pltpu.make_async_copy(ys_hbm.at[pl.ds(0, 1)], buf_ref.at[0, pl.ds(0, 1)], sem).wait()
        return carry

    lax.fori_loop(0, tc, start, 0)
    lax.fori_loop(0, 2 * tc, wait, 0)
    rt = rt_ref[...]
    ffn = rt[:, 2:3] * buf_ref[0] + rt[:, 3:4] * buf_ref[1]
    z = alpha * x_ref[...] + gate_ref[0] * ffn
    o_ref[...] = _ln_rows(z) * g_ref[...] + b_ref[...]


def _combine(d0, d1, rt, x2, mod6, mod_row, seq, ln_g, ln_b, ys, alpha):
    t, d = x2.shape
    tc = min(256, seq)
    nt = t // tc
    per = seq // tc
    smem_blk = pl.BlockSpec((1, 1, tc), lambda i: (i, 0, 0), memory_space=pltpu.SMEM)
    return pl.pallas_call(
        functools.partial(_combine_kernel, alpha=alpha, tc=tc),
        out_shape=jax.ShapeDtypeStruct((t, d), F32),
        grid=(nt,),
        in_specs=[smem_blk, smem_blk,
                  pl.BlockSpec((tc, LANES), lambda i: (i, 0)),
                  pl.BlockSpec((tc, d), lambda i: (i, 0)),
                  pl.BlockSpec((1, 1, d), lambda i: (mod_row(i // per) * 6 + 5, 0, 0)),
                  pl.BlockSpec((1, d), lambda i: (0, 0)),
                  pl.BlockSpec((1, d), lambda i: (0, 0)),
                  pl.BlockSpec(memory_space=pl.ANY)],
        out_specs=pl.BlockSpec((tc, d), lambda i: (i, 0)),
        scratch_shapes=[pltpu.VMEM((2, tc, d), F32), pltpu.SemaphoreType.DMA(())],
        compiler_params=_cparams(("arbitrary",)),
        name="moe_combine",
    )(d0.reshape(nt, 1, tc), d1.reshape(nt, 1, tc), rt, x2, mod6,
      ln_g.reshape(1, d), ln_b.reshape(1, d), ys)


def _moe_layer(xs, mod6, mod_row, w_router, w_gate, w_up, w_down, ln_g, ln_b, alpha):
    bsz, s, d = xs.shape
    t = bsz * s
    x2 = xs.reshape(t, d)
    tile = min(MOE_ROW_TILE, t)
    n_blocks = (2 * t) // tile + N_EXPERTS
    u2, rt, tot = _router(x2, mod6, mod_row, s, w_router)
    ri = rt[:, :8].astype(I32)
    tot_i = tot[0].astype(I32)
    xs_sorted, d0, d1, be = _dispatch(tot_i, ri[:, 0], ri[:, 1], ri[:, 4], ri[:, 5], u2, n_blocks, tile)
    ys = _expert_ffn(be, xs_sorted, w_gate, w_up, w_down, tile)
    out = _combine(d0, d1, rt, x2, mod6, mod_row, s, ln_g, ln_b, ys, alpha)
    return out.reshape(bsz, s, d)


def kernel(x, c, ctx, c_ctx, w_ada, b_ada, w_in, conv_w, conv_b, mlstm_gate_b, hgrn_lb,
           ret_decay_logit, head_norm_g, w_branch, w_out, post_ln_g, post_ln_b,
           ffn_w_gate, ffn_w_up, ffn_w_down, moe_w_router, moe_w_gate, moe_w_up, moe_w_down):
    bsz, s_lat, d = x.shape
    s_ctx = ctx.shape[1]
    depth = w_ada.shape[0]
    bw = d // 2
    n_gate = 4 * MLSTM_HEADS
    assert bsz < MOD_ROWS - 1 and s_lat % SCAN_CHUNK == 0 and s_ctx % SCAN_CHUNK == 0
    assert s_lat % GRID_W == 0 and (s_lat <= 1024 or s_lat % 1024 == 0)
    alpha = float((2 * depth) ** 0.25)
    ctx_row = MOD_ROWS // 2
    lat_rows = lambda b: b
    ctx_rows = lambda b: b * 0 + ctx_row

    col = {"mqk": 0, "mv": 2 * bw, "mz": 3 * bw, "hq": 4 * bw, "hi": 5 * bw, "hf": 6 * bw,
           "hg": 8 * bw, "rq": 9 * bw, "rk": 9 * bw + bw // 2, "rv": 10 * bw, "rg": 11 * bw,
           "merge": 12 * bw}
    g0 = 4 * bw

    cv = jnp.zeros((MOD_ROWS, d), F32).at[:bsz].set(c).at[ctx_row].set(c_ctx)
    mod = _ada(cv, w_ada, b_ada)

    h = ctx
    for l in range(depth):
        need_ctx = l < depth - 1
        mod6 = mod[l].reshape(MOD_ROWS * 6, 1, d)
        w_l = w_in[l]
        w_main = jnp.concatenate([w_l[:, :g0], w_l[:, g0 + n_gate:]], axis=1).astype(BF16)
        w_gate = jnp.pad(w_l[:, g0:g0 + n_gate], ((0, 0), (0, LANES - n_gate))).astype(BF16)

        y_x, g_x = _inproj(x, mod6, lat_rows, w_main, w_gate)
        y_h, g_h = _inproj(h, mod6, ctx_rows, w_main, w_gate)
        k_scale = float(bw // MLSTM_HEADS) ** -0.5
        qk = (_conv_silu(y_x, conv_w[l], conv_b[l], True, k_scale),
              _conv_silu(y_h, conv_w[l], conv_b[l], False, k_scale))
        y = (y_x, y_h)
        br_m = _mlstm(qk, y, (g_x, g_h), mlstm_gate_b[l], head_norm_g[l, 0], col["mv"], col["mz"])
        br_h = _hgrn(y, hgrn_lb, l, head_norm_g[l, 1], col["hq"], col["hi"], col["hf"], col["hg"], bw)
        br_r = _retention(y, ret_decay_logit[l], head_norm_g[l, 2],
                          col["rq"], col["rk"], col["rv"], col["rg"],
                          bw // RET_HEADS // 2, bw // RET_HEADS)
        w_br = w_branch[l].astype(BF16)
        w_o = w_out[l].astype(BF16)
        mixed = _merge_proj(br_m[0], br_h[0], br_r[0], y_x, w_br, col["merge"])
        x = _down(mixed, w_o, x, mod6, lat_rows, 2, post_ln_g[l, 0], post_ln_b[l, 0], alpha)
        if need_ctx:
            mixed_h = _merge_proj(br_m[1], br_h[1], br_r[1], y_h, w_br, col["merge"])
            h = _down(mixed_h, w_o, h, mod6, ctx_rows, 2, post_ln_g[l, 0], post_ln_b[l, 0], alpha)

        if l % 2 == 0:
            wg = ffn_w_gate[l // 2].astype(BF16)
            wu = ffn_w_up[l // 2].astype(BF16)
            wd = ffn_w_down[l // 2].astype(BF16)
            x = _down(_ffn_up(x, mod6, lat_rows, wg, wu), wd, x, mod6, lat_rows, 5,
                      post_ln_g[l, 1], post_ln_b[l, 1], alpha)
            if need_ctx:
                h = _down(_ffn_up(h, mod6, ctx_rows, wg, wu), wd, h, mod6, ctx_rows, 5,
                          post_ln_g[l, 1], post_ln_b[l, 1], alpha)
        else:
            e = l // 2
            wg = moe_w_gate[e].astype(BF16)
            wu = moe_w_up[e].astype(BF16)
            wd = moe_w_down[e].astype(BF16)
            x = _moe_layer(x, mod6, lat_rows, moe_w_router[e], wg, wu, wd,
                           post_ln_g[l, 1], post_ln_b[l, 1], alpha)
            if need_ctx:
                h = _moe_layer(h, mod6, ctx_rows, moe_w_router[e], wg, wu, wd,
                               post_ln_g[l, 1], post_ln_b[l, 1], alpha)
    return x
```

```python
import functools

import numpy as np
import jax
import jax.numpy as jnp
from jax import lax
from jax.experimental import pallas as pl
from jax.experimental.pallas import tpu as pltpu

F32 = jnp.float32
BF16 = jnp.bfloat16
I32 = jnp.int32

EPS = 1e-6
NEG = -1e30
TINY = 1e-30

LANES = 128
SUBLANES = 8
SCAN_CHUNK = 256
VMEM_LIMIT_BYTES = 56 * 1024 * 1024
GRID_W = 64
MLSTM_HEADS = 4
HGRN_HEADS = 8
HGRN_HEADS_PER_STEP = 4
RET_HEADS = 4
N_EXPERTS = 8
MOD_ROWS = 16


def _cparams(sem):
    return pltpu.CompilerParams(dimension_semantics=sem, vmem_limit_bytes=VMEM_LIMIT_BYTES)


def _dot(a, b):
    return jnp.dot(a, b, preferred_element_type=F32)


def _dot_nt(a, b):
    return lax.dot_general(a, b, (((1,), (1,)), ((), ())), preferred_element_type=F32)


def _dot_tn(a, b):
    return lax.dot_general(a, b, (((0,), (0,)), ((), ())), preferred_element_type=F32)


def _split_bf16(a):
    hi = a.astype(BF16)
    lo = (a - hi.astype(F32)).astype(BF16)
    return hi, lo


def _silu(a):
    return a * jax.nn.sigmoid(a)


def _ln_rows(a):
    mu = jnp.mean(a, axis=-1, keepdims=True)
    ac = a - mu
    var = jnp.mean(ac * ac, axis=-1, keepdims=True)
    return ac * lax.rsqrt(var + EPS)


def _ada_kernel(cv_ref, w_ref, b_ref, o_ref):
    a = _silu(cv_ref[...])
    a_hi, a_lo = _split_bf16(a)
    w_hi, w_lo = _split_bf16(w_ref[0])
    acc = _dot(a_hi, w_hi) + _dot(a_lo, w_hi) + _dot(a_hi, w_lo)
    o_ref[0] = acc + b_ref[0]


def _ada(cv, w_ada, b_ada):
    n_layers, d, n = w_ada.shape
    tn = 1024
    return pl.pallas_call(
        _ada_kernel,
        out_shape=jax.ShapeDtypeStruct((n_layers, MOD_ROWS, n), F32),
        grid=(n_layers, n // tn),
        in_specs=[pl.BlockSpec((MOD_ROWS, d), lambda l, j: (0, 0)),
                  pl.BlockSpec((1, d, tn), lambda l, j: (l, 0, j)),
                  pl.BlockSpec((1, 1, tn), lambda l, j: (l, 0, j))],
        out_specs=pl.BlockSpec((1, MOD_ROWS, tn), lambda l, j: (l, 0, j)),
        compiler_params=_cparams(("parallel", "parallel")),
        name="ada",
    )(cv, w_ada, b_ada.reshape(n_layers, 1, n))


def _inproj_kernel(x_ref, sh_ref, sc_ref, w_ref, wg_ref, y_ref, g_ref, u_ref):
    j = pl.program_id(2)

    @pl.when(j == 0)
    def _():
        u = _ln_rows(x_ref[0]) * (1.0 + sc_ref[0]) + sh_ref[0]
        ub = u.astype(BF16)
        u_ref[...] = ub
        g_ref[0] = _dot(ub, wg_ref[...])

    y_ref[0] = _dot(u_ref[...], w_ref[...]).astype(y_ref.dtype)


def _inproj(xs, mod6, mod_row, w_main, w_gate):
    bsz, s, d = xs.shape
    n = w_main.shape[1]
    tm = min(1024, s)
    tn = 1024
    row = mod_row
    return pl.pallas_call(
        _inproj_kernel,
        out_shape=(jax.ShapeDtypeStruct((bsz, s, n), BF16),
                   jax.ShapeDtypeStruct((bsz, s, LANES), F32)),
        grid=(bsz, s // tm, n // tn),
        in_specs=[pl.BlockSpec((1, tm, d), lambda b, i, j: (b, i, 0)),
                  pl.BlockSpec((1, 1, d), lambda b, i, j: (row(b) * 6 + 0, 0, 0)),
                  pl.BlockSpec((1, 1, d), lambda b, i, j: (row(b) * 6 + 1, 0, 0)),
                  pl.BlockSpec((d, tn), lambda b, i, j: (0, j)),
                  pl.BlockSpec((d, LANES), lambda b, i, j: (0, 0))],
        out_specs=(pl.BlockSpec((1, tm, tn), lambda b, i, j: (b, i, j)),
                   pl.BlockSpec((1, tm, LANES), lambda b, i, j: (b, i, 0))),
        scratch_shapes=[pltpu.VMEM((tm, d), BF16)],
        compiler_params=_cparams(("parallel", "parallel", "arbitrary")),
        name="inproj",
    )(xs, mod6, mod6, w_main, w_gate)


_CONV_PAD = 72


def _conv_kernel(a_ref, w_ref, b_ref, o_ref, pad_ref, *, seq, on_grid, k_scale, k_from):
    ct = a_ref.shape[2]
    zeros = jnp.zeros((_CONV_PAD, ct), F32)
    pad_ref[pl.ds(0, _CONV_PAD), :] = zeros
    pad_ref[pl.ds(_CONV_PAD + seq, _CONV_PAD), :] = zeros
    pad_ref[pl.ds(_CONV_PAD, seq), :] = a_ref[0].astype(F32)
    w = w_ref[...]
    bias = b_ref[...]
    scale = jnp.where(pl.program_id(1) >= k_from, k_scale, 1.0).astype(F32)

    rc = min(512, seq)
    for r0 in range(0, seq, rc):
        col = lax.broadcasted_iota(I32, (rc, 1), 0) % GRID_W
        acc = jnp.zeros((rc, ct), F32)
        for dr in range(3) if on_grid else (1,):
            for dc in range(3):
                off = (dr - 1) * GRID_W + (dc - 1)
                tap = pad_ref[pl.ds(_CONV_PAD + r0 + off, rc), :]
                if on_grid and dc == 0:
                    tap = jnp.where(col == 0, 0.0, tap)
                elif on_grid and dc == 2:
                    tap = jnp.where(col == GRID_W - 1, 0.0, tap)
                acc = acc + tap * w[dr * 3 + dc:dr * 3 + dc + 1, :]
        o_ref[0, pl.ds(r0, rc), :] = (_silu(acc + bias) * scale).astype(o_ref.dtype)


def _conv_silu(y, conv_w, conv_b, on_grid, k_scale):
    bsz, seq, _ = y.shape
    ch = conv_w.shape[-1]
    ct = 256
    kern = functools.partial(_conv_kernel, seq=seq, on_grid=on_grid, k_scale=k_scale,
                             k_from=(ch // 2) // ct)
    return pl.pallas_call(
        kern,
        out_shape=jax.ShapeDtypeStruct((bsz, seq, ch), BF16),
        grid=(bsz, ch // ct),
        in_specs=[pl.BlockSpec((1, seq, ct), lambda b, j: (b, 0, j)),
                  pl.BlockSpec((9, ct), lambda b, j: (0, j)),
                  pl.BlockSpec((1, ct), lambda b, j: (0, j))],
        out_specs=pl.BlockSpec((1, seq, ct), lambda b, j: (b, 0, j)),
        scratch_shapes=[pltpu.VMEM((seq + 2 * _CONV_PAD, ct), F32)],
        compiler_params=_cparams(("parallel", "parallel")),
        name="conv_silu",
    )(y, conv_w.reshape(9, ch), conv_b.reshape(1, ch))


def _sched(s, n_lat, n_ctx):
    nc = n_lat + n_ctx
    d = s // nc
    p = s - d * nc
    is_ctx = p < n_ctx
    pc = jnp.minimum(p, n_ctx - 1)
    pq = jnp.maximum(p - n_ctx, 0)
    ctx_c = jnp.where(d == 0, pc, n_ctx - 1 - pc)
    lat_c = jnp.where(d == 0, pq, n_lat - 1 - pq)
    return d, p, is_ctx, lat_c, ctx_c


def _scan_specs(n_lat, n_ctx):
    c = SCAN_CHUNK
    nc = n_lat + n_ctx

    def in_pair(width, col_blk):
        return (pl.BlockSpec((1, c, width), lambda b, h, s: (b, _sched(s, n_lat, n_ctx)[3], col_blk(h, s))),
                pl.BlockSpec((1, c, width), lambda b, h, s: (b, _sched(s, n_lat, n_ctx)[4], col_blk(h, s))))

    def out_pair(width, col_blk):
        return (pl.BlockSpec((1, c, width),
                             lambda b, h, s: (b, _sched(jnp.maximum(s, nc), n_lat, n_ctx)[3], col_blk(h, s))),
                pl.BlockSpec((1, c, width),
                             lambda b, h, s: (b, _sched(jnp.maximum(s, nc), n_lat, n_ctx)[4], col_blk(h, s))))

    return in_pair, out_pair


def _store_scan_out(is_ctx, val, ol_ref, oc_ref):
    @pl.when(is_ctx)
    def _():
        oc_ref[0] = val.astype(oc_ref.dtype)

    @pl.when(jnp.logical_not(is_ctx))
    def _():
        ol_ref[0] = val.astype(ol_ref.dtype)


def _log_sigmoid(a):
    return jnp.minimum(a, 0.0) - jnp.log1p(jnp.exp(-jnp.abs(a)))


def _prefix_sums(cols, rows, tri_b, nh):
    lane = lax.broadcasted_iota(I32, (1, LANES), 1)
    hi = cols.astype(BF16).astype(F32)
    rhs = jnp.where(lane < nh, hi, pltpu.roll(cols - hi, nh, axis=1)).astype(BF16)
    bc = _dot(tri_b, rhs)
    col = bc + pltpu.roll(bc, LANES - nh, axis=1)
    sub = lax.broadcasted_iota(I32, (SUBLANES, 1), 0)
    rhi = rows.astype(BF16).astype(F32)
    lhs = jnp.where(sub < nh, rhi, pltpu.roll(rows - rhi, nh, axis=0))
    lhs = jnp.concatenate([lhs, jnp.zeros_like(lhs)], axis=0).astype(BF16)
    br = _dot_nt(lhs, tri_b)[0:SUBLANES]
    row = br + pltpu.roll(br, SUBLANES - nh, axis=0)
    return col, row


def _mlstm_kernel(ql_ref, qc_ref, kl_ref, kc_ref, vl_ref, vc_ref, zl_ref, zc_ref, gl_ref, gc_ref,
                  gb_ref, ng_ref, ol_ref, oc_ref, c_ref, n_ref, m_ref, hbuf_ref, *, n_lat, n_ctx):
    nh = MLSTM_HEADS
    d, p, is_ctx, lat_c, ctx_c = _sched(pl.program_id(2), n_lat, n_ctx)
    chunk = jnp.where(is_ctx, n_lat + ctx_c, lat_c)
    pick = lambda a_l, a_c: jnp.where(is_ctx, a_c[0], a_l[0])
    c = SCAN_CHUNK
    dh = c_ref.shape[1]

    @pl.when(p == 0)
    def _():
        c_ref[...] = jnp.zeros_like(c_ref)
        n_ref[...] = jnp.zeros_like(n_ref)
        m_ref[...] = jnp.full_like(m_ref, NEG)

    gates = pick(gl_ref, gc_ref) + gb_ref[...]
    gates = jnp.where(d == 0, gates, pltpu.roll(gates, LANES - 2 * nh, axis=1))
    lane = lax.broadcasted_iota(I32, (1, LANES), 1)
    r32 = jnp.where(lane < nh, gates, jnp.where(lane < 2 * nh, _log_sigmoid(gates), 0.0))
    rt = r32.T[0:SUBLANES]
    lf_cols = jnp.where(lane < nh, pltpu.roll(r32, LANES - nh, axis=1), 0.0)
    sub = lax.broadcasted_iota(I32, (SUBLANES, 1), 0)
    lf_rows = jnp.where(sub < nh, pltpu.roll(rt, SUBLANES - nh, axis=0), 0.0)

    t_i = lax.broadcasted_iota(I32, (c, c), 0)
    s_i = lax.broadcasted_iota(I32, (c, c), 1)
    tri = (t_i - s_i) * (1 - 2 * d) >= 0
    b_cols, b_rows = _prefix_sums(lf_cols, lf_rows, tri.astype(BF16), nh)
    totals = jnp.sum(lf_cols, axis=0, keepdims=True)

    q_all = pick(ql_ref, qc_ref)
    k_all = pick(kl_ref, kc_ref)
    v_all = pick(vl_ref, vc_ref)
    outs = []
    for j in range(nh):
        li = r32[:, j:j + 1]
        li_row = rt[j:j + 1, :]
        b_col = b_cols[:, j:j + 1]
        b_row = b_rows[j:j + 1, :]
        total = totals[:, j:j + 1]
        q = q_all[:, j * dh:(j + 1) * dh]
        k = k_all[:, j * dh:(j + 1) * dh]
        vb = v_all[:, j * dh:(j + 1) * dh].astype(BF16)

        m_prev = m_ref[j, 0:1, 0:1]
        d_mat = jnp.where(tri, b_col - b_row + li_row, NEG)
        a_inter = b_col + m_prev
        m_t = jnp.maximum(a_inter, jnp.max(d_mat, axis=1, keepdims=True))
        w_inter = jnp.exp(a_inter - m_t)
        s_mat = _dot_nt(q, k) * jnp.exp(d_mat - m_t)
        num = w_inter * _dot(q, c_ref[j].astype(BF16)) + _dot(s_mat.astype(BF16), vb)
        den = (w_inter * jnp.sum(q.astype(F32) * n_ref[j, 0:1, :], axis=1, keepdims=True)
               + jnp.sum(s_mat, axis=1, keepdims=True))
        outs.append(num / jnp.maximum(jnp.abs(den), jnp.exp(-m_t)))

        g_col = total - b_col + li
        m_new = jnp.maximum(total + m_prev, jnp.max(g_col, axis=0, keepdims=True))
        decay = jnp.exp(total + m_prev - m_new)
        kw = k.astype(F32) * jnp.exp(g_col - m_new)
        c_ref[j] = decay * c_ref[j] + _dot_tn(kw.astype(BF16), vb)
        n_new = decay * n_ref[j, 0:1, :] + jnp.sum(kw, axis=0, keepdims=True)
        n_ref[j] = jnp.broadcast_to(n_new, n_ref.shape[1:])
        m_ref[j] = jnp.broadcast_to(m_new, m_ref.shape[1:])

    @pl.when(d == 0)
    def _():
        for j in range(nh):
            hbuf_ref[chunk, :, j * dh:(j + 1) * dh] = outs[j]

    @pl.when(d == 1)
    def _():
        z = pick(zl_ref, zc_ref).astype(F32)
        fin = []
        for j in range(nh):
            sl = slice(j * dh, (j + 1) * dh)
            hn = _ln_rows(hbuf_ref[chunk, :, sl] + outs[j])
            fin.append(hn * ng_ref[:, sl] * _silu(z[:, sl]))
        _store_scan_out(is_ctx, jnp.concatenate(fin, axis=1), ol_ref, oc_ref)


def _mlstm(qk, y, g, gate_b, norm_g, v_col, z_col):
    bsz, s_lat, w2 = qk[0].shape
    s_ctx = qk[1].shape[1]
    width = w2 // 2
    dh = width // MLSTM_HEADS
    c = SCAN_CHUNK
    n_lat, n_ctx = s_lat // c, s_ctx // c
    in_pair, out_pair = _scan_specs(n_lat, n_ctx)
    kern = functools.partial(_mlstm_kernel, n_lat=n_lat, n_ctx=n_ctx)
    gb = jnp.zeros((1, LANES), F32).at[0, :gate_b.shape[0]].set(gate_b)
    return pl.pallas_call(
        kern,
        out_shape=(jax.ShapeDtypeStruct((bsz, s_lat, width), BF16),
                   jax.ShapeDtypeStruct((bsz, s_ctx, width), BF16)),
        grid=(bsz, 1, 2 * (n_lat + n_ctx)),
        in_specs=[*in_pair(width, lambda h, s: 0),
                  *in_pair(width, lambda h, s: 1),
                  *in_pair(width, lambda h, s: v_col // width),
                  *out_pair(width, lambda h, s: z_col // width),
                  *in_pair(LANES, lambda h, s: 0),
                  pl.BlockSpec((1, LANES), lambda b, h, s: (0, 0)),
                  pl.BlockSpec((1, width), lambda b, h, s: (0, 0))],
        out_specs=out_pair(width, lambda h, s: 0),
        scratch_shapes=[pltpu.VMEM((MLSTM_HEADS, dh, dh), F32),
                        pltpu.VMEM((MLSTM_HEADS, SUBLANES, dh), F32),
                        pltpu.VMEM((MLSTM_HEADS, SUBLANES, LANES), F32),
                        pltpu.VMEM((n_lat + n_ctx, c, width), F32)],
        compiler_params=_cparams(("parallel", "parallel", "arbitrary")),
        name="mlstm",
    )(qk[0], qk[1], qk[0], qk[1], y[0], y[1], y[0], y[1], g[0], g[1], gb, norm_g.reshape(1, width))


def _ret_kernel(ql_ref, qc_ref, kl_ref, kc_ref, vl_ref, vc_ref, zl_ref, zc_ref, dl_ref, ng_ref,
                ol_ref, oc_ref, s_ref, dmat_ref, hbuf_ref, *, n_lat, n_ctx, q_scale):
    nh = RET_HEADS
    d, p, is_ctx, lat_c, ctx_c = _sched(pl.program_id(2), n_lat, n_ctx)
    chunk = jnp.where(is_ctx, n_lat + ctx_c, lat_c)
    pick = lambda a_l, a_c: jnp.where(is_ctx, a_c[0], a_l[0])
    c = SCAN_CHUNK
    dk, dv = s_ref.shape[1], s_ref.shape[2]

    lane = lax.broadcasted_iota(I32, (1, LANES), 1)
    lgs = _log_sigmoid(dl_ref[...])
    lg = [jnp.sum(jnp.where(lane == d * nh + j, lgs, 0.0), axis=1, keepdims=True) for j in range(nh)]

    @pl.when(p == 0)
    def _():
        s_ref[...] = jnp.zeros_like(s_ref)
        t_i = lax.broadcasted_iota(I32, (c, c), 0)
        s_i = lax.broadcasted_iota(I32, (c, c), 1)
        dist = (t_i - s_i) * (1 - 2 * d)
        distf = jnp.maximum(dist, 0).astype(F32)
        for j in range(nh):
            dmat_ref[j] = jnp.where(dist >= 0, jnp.exp(lg[j] * distf), 0.0)

    t_c = lax.broadcasted_iota(I32, (c, 1), 0)
    pos = jnp.where(d == 0, t_c + 1, c - t_c).astype(F32)
    q_all = pick(ql_ref, qc_ref)
    k_all = pick(kl_ref, kc_ref)
    v_all = pick(vl_ref, vc_ref)
    outs = []
    for j in range(nh):
        qb = (q_all[:, j * dk:(j + 1) * dk].astype(F32) * q_scale).astype(BF16)
        kf = k_all[:, j * dk:(j + 1) * dk].astype(F32)
        vb = v_all[:, j * dv:(j + 1) * dv].astype(BF16)
        a = _dot_nt(qb, kf.astype(BF16)) * dmat_ref[j]
        outs.append(jnp.exp(lg[j] * pos) * _dot(qb, s_ref[j].astype(BF16)) + _dot(a.astype(BF16), vb))
        kw = kf * jnp.exp(lg[j] * (c - pos))
        s_ref[j] = jnp.exp(lg[j] * c) * s_ref[j] + _dot_tn(kw.astype(BF16), vb)

    @pl.when(d == 0)
    def _():
        for j in range(nh):
            hbuf_ref[chunk, :, j * dv:(j + 1) * dv] = outs[j]

    @pl.when(d == 1)
    def _():
        z = pick(zl_ref, zc_ref).astype(F32)
        fin = []
        for j in range(nh):
            sl = slice(j * dv, (j + 1) * dv)
            hn = _ln_rows(hbuf_ref[chunk, :, sl] + outs[j])
            fin.append(hn * ng_ref[:, sl] * _silu(z[:, sl]))
        _store_scan_out(is_ctx, jnp.concatenate(fin, axis=1), ol_ref, oc_ref)


def _retention(y, decay_logit, norm_g, q_col, k_col, v_col, z_col, dk, dv):
    bsz, s_lat, _ = y[0].shape
    s_ctx = y[1].shape[1]
    c = SCAN_CHUNK
    n_lat, n_ctx = s_lat // c, s_ctx // c
    in_pair, out_pair = _scan_specs(n_lat, n_ctx)
    kern = functools.partial(_ret_kernel, n_lat=n_lat, n_ctx=n_ctx, q_scale=float(dk) ** -0.5)
    wk, wv = RET_HEADS * dk, RET_HEADS * dv
    dl = jnp.zeros((1, LANES), F32).at[0, :2 * RET_HEADS].set(decay_logit.reshape(-1))
    return pl.pallas_call(
        kern,
        out_shape=(jax.ShapeDtypeStruct((bsz, s_lat, RET_HEADS * dv), BF16),
                   jax.ShapeDtypeStruct((bsz, s_ctx, RET_HEADS * dv), BF16)),
        grid=(bsz, 1, 2 * (n_lat + n_ctx)),
        in_specs=[*in_pair(wk, lambda h, s: q_col // wk),
                  *in_pair(wk, lambda h, s: k_col // wk),
                  *in_pair(wv, lambda h, s: v_col // wv),
                  *out_pair(wv, lambda h, s: z_col // wv),
                  pl.BlockSpec((1, LANES), lambda b, h, s: (0, 0)),
                  pl.BlockSpec((1, wv), lambda b, h, s: (0, 0))],
        out_specs=out_pair(wv, lambda h, s: 0),
        scratch_shapes=[pltpu.VMEM((RET_HEADS, dk, dv), F32),
                        pltpu.VMEM((RET_HEADS, c, c), F32),
                        pltpu.VMEM((n_lat + n_ctx, c, wv), F32)],
        compiler_params=_cparams(("parallel", "parallel", "arbitrary")),
        name="retention",
    )(y[0], y[1], y[0], y[1], y[0], y[1], y[0], y[1], dl, norm_g.reshape(1, RET_HEADS * dv))


def _hgrn_tables(c):
    nl = int(np.log2(c))
    t = np.arange(c)[:, None]
    u = np.arange(c)[None, :]
    x = (t ^ u).astype(np.int64)
    lvl = np.where(x == 0, nl, np.floor(np.log2(np.maximum(x, 1))).astype(np.int64))
    return (jnp.asarray((u <= t).astype(np.float32), BF16), jnp.asarray(lvl, I32), nl)


def _block_ref_rows(b, lev):
    c, w = b.shape
    n = 1 << lev
    if 2 * n >= SUBLANES:
        b3 = b.reshape(c // (2 * n), 2 * n, w)
        return jnp.broadcast_to(b3[:, n - 1:n, :], b3.shape).reshape(c, w)
    t = lax.broadcasted_iota(I32, (c, 1), 0) & (2 * n - 1)
    out = b
    for k in range(2 * n):
        if k != n - 1:
            out = jnp.where(t == k, pltpu.roll(b, (k - (n - 1)) % c, axis=0), out)
    return out


def _hgrn_kernel(ql_ref, qc_ref, vl_ref, vc_ref, fl_ref, fc_ref, zl_ref, zc_ref, lb_ref, ng_ref,
                 tab_ref, lvl_ref, ol_ref, oc_ref, st_ref, hbuf_ref, *, n_lat, n_ctx, layer, nl):
    d, p, is_ctx, lat_c, ctx_c = _sched(pl.program_id(2), n_lat, n_ctx)
    chunk = jnp.where(is_ctx, n_lat + ctx_c, lat_c)
    pick = lambda a_l, a_c: jnp.where(is_ctx, a_c[0], a_l[0])
    c = SCAN_CHUNK

    @pl.when(p == 0)
    def _():
        st_ref[...] = jnp.zeros_like(st_ref)

    lbp = lb_ref[0]
    sm = jnp.exp(lbp - jnp.max(lbp, axis=0, keepdims=True))
    sm = sm / jnp.sum(sm, axis=0, keepdims=True)
    lb = jnp.zeros((1, lbp.shape[1]), F32)
    for i in range(1, layer + 1):
        lb = lb + sm[i:i + 1, :]

    ft = pick(fl_ref, fc_ref).astype(F32)
    kk = (1.0 - lb) * jax.nn.sigmoid(-ft)
    lf = jnp.log(jnp.maximum(lb + (1.0 - lb) * jax.nn.sigmoid(ft), TINY))
    q = _silu(pick(ql_ref, qc_ref).astype(F32))
    v_all = pick(vl_ref, vc_ref)
    w = lf.shape[1]
    dk = st_ref.shape[1]
    nh = w // dk
    lf_hi = lf.astype(BF16)
    r1 = lf - lf_hi.astype(F32)
    lf_mid = r1.astype(BF16)
    lf_lo = (r1 - lf_mid.astype(F32)).astype(BF16)
    ps = _dot(tab_ref[...], jnp.concatenate([lf_hi, lf_mid, lf_lo], axis=1))
    b = ps[:, :w] + ps[:, w:2 * w] + ps[:, 2 * w:]
    total = jnp.sum(lf, axis=0, keepdims=True)
    bx = b - jnp.where(d == 0, 0.0, 1.0) * lf

    def seg_exp(idx):
        if idx < nl:
            return jnp.exp(-jnp.abs(bx - _block_ref_rows(b, idx)))
        query_side = (idx == nl)
        from_start = jnp.where(d == 0, 1.0, 0.0) if query_side else jnp.where(d == 0, 0.0, 1.0)
        return jnp.exp(from_start * bx + (1.0 - from_start) * (total - bx))

    t_c = lax.broadcasted_iota(I32, (c, 1), 0)
    lvl = lvl_ref[...]
    hs = [slice(j * dk, (j + 1) * dk) for j in range(nh)]
    qb = q.astype(BF16)
    kb = kk.astype(BF16)
    accs = [jnp.where(lvl == nl, _dot_nt(qb[:, sl], kb[:, sl]), 0.0) for sl in hs]
    for lev in range(nl):
        e = seg_exp(lev)
        is_q = ((t_c >> lev) & 1) != d
        qt = jnp.where(is_q, q * e, 0.0).astype(BF16)
        kt = jnp.where(is_q, 0.0, kk * e).astype(BF16)
        hit = lvl == lev
        accs = [jnp.where(hit, _dot_nt(qt[:, sl], kt[:, sl]), a) for sl, a in zip(hs, accs)]

    qg = (q * seg_exp(nl)).astype(BF16)
    kg = (kk * seg_exp(nl + 1)).astype(BF16)
    decay = jnp.exp(jnp.sum(lf, axis=0, keepdims=True))
    outs = []
    for j, sl in enumerate(hs):
        st = st_ref[j]
        vb = v_all[:, sl].astype(BF16)
        outs.append(_dot(accs[j].astype(BF16), vb) + _dot_nt(qg[:, sl], st.astype(BF16)))
        st_ref[j] = st * decay[:, sl] + _dot_tn(vb, kg[:, sl])

    @pl.when(d == 0)
    def _():
        for j, sl in enumerate(hs):
            hbuf_ref[chunk, :, sl] = outs[j]

    @pl.when(d == 1)
    def _():
        z = pick(zl_ref, zc_ref).astype(F32)
        fin = []
        for j, sl in enumerate(hs):
            tot = hbuf_ref[chunk, :, sl] + outs[j]
            hn = tot * lax.rsqrt(jnp.mean(tot * tot, axis=-1, keepdims=True) + EPS)
            fin.append(hn * ng_ref[:, sl] * _silu(z[:, sl]))
        _store_scan_out(is_ctx, jnp.concatenate(fin, axis=1), ol_ref, oc_ref)


def _hgrn(y, hgrn_lb, layer, norm_g, q_col, v_col, f_col, z_col, width):
    bsz, s_lat, _ = y[0].shape
    s_ctx = y[1].shape[1]
    dk = width // HGRN_HEADS
    depth = hgrn_lb.shape[1]
    c = SCAN_CHUNK
    n_lat, n_ctx = s_lat // c, s_ctx // c
    nc = n_lat + n_ctx
    tabs, lvl, nl = _hgrn_tables(c)
    in_pair, out_pair = _scan_specs(n_lat, n_ctx)
    kern = functools.partial(_hgrn_kernel, n_lat=n_lat, n_ctx=n_ctx, layer=layer, nl=nl)
    hp = HGRN_HEADS_PER_STEP
    wg = hp * dk
    per_dir = width // wg
    return pl.pallas_call(
        kern,
        out_shape=(jax.ShapeDtypeStruct((bsz, s_lat, width), BF16),
                   jax.ShapeDtypeStruct((bsz, s_ctx, width), BF16)),
        grid=(bsz, HGRN_HEADS // hp, 2 * nc),
        in_specs=[*in_pair(wg, lambda h, s: q_col // wg + h),
                  *in_pair(wg, lambda h, s: v_col // wg + h),
                  *in_pair(wg, lambda h, s: f_col // wg + (s // nc) * per_dir + h),
                  *out_pair(wg, lambda h, s: z_col // wg + h),
                  pl.BlockSpec((1, depth, wg), lambda b, h, s: (s // nc, 0, h)),
                  pl.BlockSpec((1, wg), lambda b, h, s: (0, h)),
                  pl.BlockSpec((c, c), lambda b, h, s: (0, 0)),
                  pl.BlockSpec((c, c), lambda b, h, s: (0, 0))],
        out_specs=out_pair(wg, lambda h, s: h),
        scratch_shapes=[pltpu.VMEM((hp, dk, dk), F32),
                        pltpu.VMEM((nc, c, wg), F32)],
        compiler_params=_cparams(("parallel", "parallel", "arbitrary")),
        name="hgrn2",
    )(y[0], y[1], y[0], y[1], y[0], y[1], y[0], y[1], hgrn_lb, norm_g.reshape(1, width), tabs, lvl)


def _merge_kernel(bm_ref, bh_ref, br_ref, g0_ref, g1_ref, g2_ref, w_ref, o_ref):
    acc = jax.nn.sigmoid(g0_ref[0].astype(F32)) * _dot(bm_ref[0], w_ref[0])
    acc = acc + jax.nn.sigmoid(g1_ref[0].astype(F32)) * _dot(bh_ref[0], w_ref[1])
    acc = acc + jax.nn.sigmoid(g2_ref[0].astype(F32)) * _dot(br_ref[0], w_ref[2])
    o_ref[0] = acc.astype(o_ref.dtype)


def _merge_proj(br_m, br_h, br_r, y, w_br, merge_col):
    bsz, s, width = br_m.shape
    d = w_br.shape[2]
    tm = min(1024, s)
    tn = 512
    gcol = merge_col // tn
    nj = d // tn
    bspec = pl.BlockSpec((1, tm, width), lambda b, i, j: (b, i, 0))

    def gspec(n):
        return pl.BlockSpec((1, tm, tn), lambda b, i, j: (b, i, gcol + n * nj + j))

    return pl.pallas_call(
        _merge_kernel,
        out_shape=jax.ShapeDtypeStruct((bsz, s, d), BF16),
        grid=(bsz, s // tm, nj),
        in_specs=[bspec, bspec, bspec, gspec(0), gspec(1), gspec(2),
                  pl.BlockSpec((3, width, tn), lambda b, i, j: (0, 0, j))],
        out_specs=pl.BlockSpec((1, tm, tn), lambda b, i, j: (b, i, j)),
        compiler_params=_cparams(("parallel", "parallel", "arbitrary")),
        name="merge_proj",
    )(br_m, br_h, br_r, y, y, y, w_br)


def _down_kernel(a_ref, w_ref, x_ref, gate_ref, g_ref, b_ref, o_ref, *, alpha, nk):
    k = pl.program_id(2)

    @pl.when(k == 0)
    def _():
        o_ref[...] = jnp.zeros_like(o_ref)

    o_ref[0] += _dot(a_ref[0], w_ref[...])

    @pl.when(k == nk - 1)
    def _():
        z = alpha * x_ref[0] + gate_ref[0] * o_ref[0]
        o_ref[0] = _ln_rows(z) * g_ref[...] + b_ref[...]


def _down(a, w, xs, mod6, mod_row, which, ln_g, ln_b, alpha):
    bsz, s, d = xs.shape
    kdim = w.shape[0]
    tk = next(t for t in (2048, 1408, 1024, 512, kdim) if kdim % t == 0)
    nk = kdim // tk
    tm = min(1024 if nk == 1 else 512, s)
    row = mod_row
    kern = functools.partial(_down_kernel, alpha=alpha, nk=nk)
    return pl.pallas_call(
        kern,
        out_shape=jax.ShapeDtypeStruct((bsz, s, d), F32),
        grid=(bsz, s // tm, nk),
        in_specs=[pl.BlockSpec((1, tm, tk), lambda b, i, k: (b, i, k)),
                  pl.BlockSpec((tk, d), lambda b, i, k: (k, 0)),
                  pl.BlockSpec((1, tm, d), lambda b, i, k: (b, i, 0)),
                  pl.BlockSpec((1, 1, d), lambda b, i, k: (row(b) * 6 + which, 0, 0)),
                  pl.BlockSpec((1, d), lambda b, i, k: (0, 0)),
                  pl.BlockSpec((1, d), lambda b, i, k: (0, 0))],
        out_specs=pl.BlockSpec((1, tm, d), lambda b, i, k: (b, i, 0)),
        compiler_params=_cparams(("parallel", "parallel", "arbitrary")),
        name="down_postnorm",
    )(a, w, xs, mod6, ln_g.reshape(1, d), ln_b.reshape(1, d))


def _ffn_up_kernel(x_ref, sh_ref, sc_ref, wg_ref, wu_ref, h_ref, u_ref):
    @pl.when(pl.program_id(2) == 0)
    def _():
        u = _ln_rows(x_ref[0]) * (1.0 + sc_ref[0]) + sh_ref[0]
        u_ref[...] = u.astype(BF16)

    ub = u_ref[...]
    h_ref[0] = (_silu(_dot(ub, wg_ref[...])) * _dot(ub, wu_ref[...])).astype(h_ref.dtype)


def _ffn_up(xs, mod6, mod_row, w_gate, w_up):
    bsz, s, d = xs.shape
    f = w_gate.shape[1]
    tm = min(1024, s)
    tn = 512
    row = mod_row
    return pl.pallas_call(
        _ffn_up_kernel,
        out_shape=jax.ShapeDtypeStruct((bsz, s, f), BF16),
        grid=(bsz, s // tm, f // tn),
        in_specs=[pl.BlockSpec((1, tm, d), lambda b, i, j: (b, i, 0)),
                  pl.BlockSpec((1, 1, d), lambda b, i, j: (row(b) * 6 + 3, 0, 0)),
                  pl.BlockSpec((1, 1, d), lambda b, i, j: (row(b) * 6 + 4, 0, 0)),
                  pl.BlockSpec((d, tn), lambda b, i, j: (0, j)),
                  pl.BlockSpec((d, tn), lambda b, i, j: (0, j))],
        out_specs=pl.BlockSpec((1, tm, tn), lambda b, i, j: (b, i, j)),
        scratch_shapes=[pltpu.VMEM((tm, d), BF16)],
        compiler_params=_cparams(("parallel", "parallel", "arbitrary")),
        name="ffn_up",
    )(xs, mod6, mod6, w_gate, w_up)


MOE_ROW_TILE = 512


def _router_kernel(x_ref, sh_ref, sc_ref, w_ref, u_ref, rt_ref, tot_ref, carry_ref, *, n_exp):
    i = pl.program_id(0)

    @pl.when(i == 0)
    def _():
        carry_ref[...] = jnp.zeros_like(carry_ref)

    u = _ln_rows(x_ref[...]) * (1.0 + sc_ref[0]) + sh_ref[0]
    u_ref[...] = u
    u_hi, u_lo = _split_bf16(u)
    w_hi, w_lo = _split_bf16(w_ref[...])
    logits = _dot(u_hi, w_hi) + _dot(u_lo, w_hi) + _dot(u_hi, w_lo)
    tr = logits.shape[0]
    lane = lax.broadcasted_iota(I32, (1, LANES), 1)
    ninf = jnp.float32(-jnp.inf)
    lg = jnp.where(lane < n_exp, logits, ninf)
    m1 = jnp.max(lg, axis=1, keepdims=True)
    i1 = jnp.min(jnp.where(lg == m1, lane, LANES), axis=1, keepdims=True)
    lg2 = jnp.where(lane == i1, ninf, lg)
    m2 = jnp.max(lg2, axis=1, keepdims=True)
    i2 = jnp.min(jnp.where(lg2 == m2, lane, LANES), axis=1, keepdims=True)
    e = jnp.exp(m2 - m1)
    w0 = 1.0 / (1.0 + e)
    w1 = e / (1.0 + e)

    oh = ((lane == i1) | (lane == i2)).astype(F32)
    r_i = lax.broadcasted_iota(I32, (tr, tr), 0)
    c_i = lax.broadcasted_iota(I32, (tr, tr), 1)
    cum = _dot((c_i < r_i).astype(BF16), oh.astype(BF16)) + carry_ref[0:1, :]
    r0 = jnp.sum(jnp.where(lane == i1, cum, 0.0), axis=1, keepdims=True)
    r1 = jnp.sum(jnp.where(lane == i2, cum, 0.0), axis=1, keepdims=True)
    new_carry = carry_ref[0:1, :] + jnp.sum(oh, axis=0, keepdims=True)
    carry_ref[...] = jnp.broadcast_to(new_carry, carry_ref.shape)
    tot_ref[...] = jnp.broadcast_to(new_carry, tot_ref.shape)

    rt = jnp.where(lane == 0, i1.astype(F32), 0.0)
    rt = jnp.where(lane == 1, i2.astype(F32), rt)
    rt = jnp.where(lane == 2, w0, rt)
    rt = jnp.where(lane == 3, w1, rt)
    rt = jnp.where(lane == 4, r0, rt)
    rt = jnp.where(lane == 5, r1, rt)
    rt_ref[...] = rt


def _router(x2, mod6, mod_row, seq, w_router):
    t, d = x2.shape
    n_exp = w_router.shape[1]
    tr = min(512, seq)
    wr = jnp.zeros((d, LANES), F32).at[:, :n_exp].set(w_router)
    kern = functools.partial(_router_kernel, n_exp=n_exp)
    per = seq // tr
    return pl.pallas_call(
        kern,
        out_shape=(jax.ShapeDtypeStruct((t, d), F32),
                   jax.ShapeDtypeStruct((t, LANES), F32),
                   jax.ShapeDtypeStruct((SUBLANES, LANES), F32)),
        grid=(t // tr,),
        in_specs=[pl.BlockSpec((tr, d), lambda i: (i, 0)),
                  pl.BlockSpec((1, 1, d), lambda i: (mod_row(i // per) * 6 + 3, 0, 0)),
                  pl.BlockSpec((1, 1, d), lambda i: (mod_row(i // per) * 6 + 4, 0, 0)),
                  pl.BlockSpec((d, LANES), lambda i: (0, 0))],
        out_specs=(pl.BlockSpec((tr, d), lambda i: (i, 0)),
                   pl.BlockSpec((tr, LANES), lambda i: (i, 0)),
                   pl.BlockSpec((SUBLANES, LANES), lambda i: (0, 0))),
        scratch_shapes=[pltpu.VMEM((SUBLANES, LANES), F32)],
        compiler_params=_cparams(("arbitrary",)),
        name="moe_router",
    )(x2, mod6, mod6, wr)


def _dispatch_kernel(tot_ref, e0_ref, e1_ref, r0_ref, r1_ref, u_ref,
                     xs_hbm, d0_ref, d1_ref, be_ref, gs_ref, zero_ref, sem, zsem,
                     *, n_exp, n_blocks, tile, td):
    i = pl.program_id(0)

    @pl.when(i == 0)
    def _():
        start = jnp.int32(0)
        for e in range(n_exp):
            gs_ref[e] = start
            start = start + ((tot_ref[e] + (tile - 1)) // tile) * tile
        gs_ref[n_exp] = start

        def blk(b, carry):
            row = b * tile
            ex = jnp.int32(0)
            for e in range(1, n_exp):
                ex = jnp.where(row >= gs_ref[e], e, ex)
            be_ref[0, b] = ex
            be_ref[1, b] = jnp.where(row < gs_ref[n_exp], 1, 0).astype(I32)
            return carry

        lax.fori_loop(0, n_blocks, blk, 0)

        zero_ref[...] = jnp.zeros_like(zero_ref)
        for e in range(n_exp):
            last = pl.multiple_of(jnp.maximum(gs_ref[e + 1] - tile, 0), tile)
            pltpu.make_async_copy(zero_ref, xs_hbm.at[pl.ds(last, tile)], zsem).start()
        for e in range(n_exp):
            pltpu.make_async_copy(zero_ref, xs_hbm.at[pl.ds(0, tile)], zsem).wait()

        first_unused = gs_ref[n_exp] // tile

        def ztail(b, carry):
            pltpu.make_async_copy(zero_ref, xs_hbm.at[pl.ds(pl.multiple_of(b * tile, tile), tile)], zsem).start()
            return carry

        def ztail_wait(b, carry):
            pltpu.make_async_copy(zero_ref, xs_hbm.at[pl.ds(0, tile)], zsem).wait()
            return carry

        lax.fori_loop(first_unused, n_blocks, ztail, 0)
        lax.fori_loop(first_unused, n_blocks, ztail_wait, 0)

    def row(t, carry):
        da = gs_ref[e0_ref[0, 0, t]] + r0_ref[0, 0, t]
        db = gs_ref[e1_ref[0, 0, t]] + r1_ref[0, 0, t]
        d0_ref[0, 0, t] = da
        d1_ref[0, 0, t] = db
        pltpu.make_async_copy(u_ref.at[pl.ds(t, 1)], xs_hbm.at[pl.ds(da, 1)], sem).start(priority=0)
        pltpu.make_async_copy(u_ref.at[pl.ds(t, 1)], xs_hbm.at[pl.ds(db, 1)], sem).start(priority=1)
        return carry

    lax.fori_loop(0, td, row, 0, unroll=4)
    for _ in range(2):
        pltpu.make_async_copy(u_ref, xs_hbm.at[pl.ds(0, td)], sem).wait()


def _dispatch(tot_i, e0, e1, r0, r1, u2, n_blocks, tile):
    t, d = u2.shape
    td = min(512, t)
    n_exp = N_EXPERTS
    nt = t // td
    resh = lambda a: a.reshape(nt, 1, td)
    smem_blk = pl.BlockSpec((1, 1, td), lambda i, tot: (i, 0, 0), memory_space=pltpu.SMEM)
    kern = functools.partial(_dispatch_kernel, n_exp=n_exp, n_blocks=n_blocks, tile=tile, td=td)
    grid_spec = pltpu.PrefetchScalarGridSpec(
        num_scalar_prefetch=1,
        grid=(nt,),
        in_specs=[smem_blk, smem_blk, smem_blk, smem_blk,
                  pl.BlockSpec((td, d), lambda i, tot: (i, 0))],
        out_specs=(pl.BlockSpec(memory_space=pl.ANY), smem_blk, smem_blk,
                   pl.BlockSpec((2, n_blocks), lambda i, tot: (0, 0), memory_space=pltpu.SMEM)),
        scratch_shapes=[pltpu.SMEM((n_exp + 1,), I32),
                        pltpu.VMEM((tile, d), F32),
                        pltpu.SemaphoreType.DMA(()),
                        pltpu.SemaphoreType.DMA(())],
    )
    xs, d0, d1, be = pl.pallas_call(
        kern,
        out_shape=(jax.ShapeDtypeStruct((n_blocks * tile, d), F32),
                   jax.ShapeDtypeStruct((nt, 1, td), I32),
                   jax.ShapeDtypeStruct((nt, 1, td), I32),
                   jax.ShapeDtypeStruct((2, n_blocks), I32)),
        grid_spec=grid_spec,
        compiler_params=_cparams(("arbitrary",)),
        name="moe_dispatch",
    )(tot_i, resh(e0), resh(e1), resh(r0), resh(r1), u2)
    return xs, d0.reshape(t), d1.reshape(t), be


def _expert_kernel(be_ref, x_ref, wg_ref, wu_ref, wd_ref, y_ref, xb_ref):
    b = pl.program_id(0)
    f = pl.program_id(1)
    valid = be_ref[1, b] == 1

    @pl.when(f == 0)
    def _():
        y_ref[...] = jnp.zeros_like(y_ref)

    @pl.when(valid)
    def _():
        @pl.when(f == 0)
        def _():
            xb_ref[...] = x_ref[...].astype(BF16)

        xb = xb_ref[...]
        hmid = _silu(_dot(xb, wg_ref[0])) * _dot(xb, wu_ref[0])
        y_ref[...] += _dot(hmid.astype(BF16), wd_ref[0])


def _expert_ffn(be, xs, w_gate, w_up, w_down, tile):
    rows, d = xs.shape
    n_blocks = rows // tile
    fdim = w_gate.shape[2]
    tf = 1024
    nf = fdim // tf

    def fsel(b, f, be_ref):
        return jnp.where(be_ref[1, b] == 1, f, nf - 1)

    grid_spec = pltpu.PrefetchScalarGridSpec(
        num_scalar_prefetch=1,
        grid=(n_blocks, nf),
        in_specs=[pl.BlockSpec((tile, d), lambda b, f, be_ref: (b * be_ref[1, b], 0)),
                  pl.BlockSpec((1, d, tf), lambda b, f, be_ref: (be_ref[0, b], 0, fsel(b, f, be_ref))),
                  pl.BlockSpec((1, d, tf), lambda b, f, be_ref: (be_ref[0, b], 0, fsel(b, f, be_ref))),
                  pl.BlockSpec((1, tf, d), lambda b, f, be_ref: (be_ref[0, b], fsel(b, f, be_ref), 0))],
        out_specs=pl.BlockSpec((tile, d), lambda b, f, be_ref: (b, 0)),
        scratch_shapes=[pltpu.VMEM((tile, d), BF16)],
    )
    return pl.pallas_call(
        _expert_kernel,
        out_shape=jax.ShapeDtypeStruct((rows, d), F32),
        grid_spec=grid_spec,
        compiler_params=_cparams(("parallel", "arbitrary")),
        name="moe_expert_ffn",
    )(be, xs, w_gate, w_up, w_down)


def _combine_kernel(d0_ref, d1_ref, n0_ref, n1_ref, rt_ref, x_ref, gate_ref, g_ref, b_ref, ys_hbm,
                    o_ref, buf_ref, sem, *, alpha, tc, nt):
    i = pl.program_id(0)
    slot = i % 2

    def gather(a_ref, b_ref2, s):
        def start(t, carry):
            pltpu.make_async_copy(ys_hbm.at[pl.ds(a_ref[0, 0, t], 1)],
                                  buf_ref.at[s, 0, pl.ds(t, 1)], sem.at[s]).start(priority=0)
            pltpu.make_async_copy(ys_hbm.at[pl.ds(b_ref2[0, 0, t], 1)],
                                  buf_ref.at[s, 1, pl.ds(t, 1)], sem.at[s]).start(priority=1)
            return carry

        lax.fori_loop(0, tc, start, 0, unroll=4)

    @pl.when(i == 0)
    def _():
        gather(d0_ref, d1_ref, 0)

    @pl.when(i + 1 < nt)
    def _():
        gather(n0_ref, n1_ref, 1 - slot)

    for half in range(2):
        pltpu.make_async_copy(ys_hbm.at[pl.ds(0, tc)], buf_ref.at[slot, half], sem.at[slot]).wait()
    rt = rt_ref[...]
    ffn = rt[:, 2:3] * buf_ref[slot, 0] + rt[:, 3:4] * buf_ref[slot, 1]
    z = alpha * x_ref[...] + gate_ref[0] * ffn
    o_ref[...] = _ln_rows(z) * g_ref[...] + b_ref[...]


def _combine(d0, d1, rt, x2, mod6, mod_row, seq, ln_g, ln_b, ys, alpha):
    t, d = x2.shape
    tc = min(256, seq)
    nt = t // tc
    per = seq // tc
    smem_blk = pl.BlockSpec((1, 1, tc), lambda i: (i, 0, 0), memory_space=pltpu.SMEM)
    next_blk = pl.BlockSpec((1, 1, tc), lambda i: (jnp.minimum(i + 1, nt - 1), 0, 0),
                            memory_space=pltpu.SMEM)
    d0r, d1r = d0.reshape(nt, 1, tc), d1.reshape(nt, 1, tc)
    return pl.pallas_call(
        functools.partial(_combine_kernel, alpha=alpha, tc=tc, nt=nt),
        out_shape=jax.ShapeDtypeStruct((t, d), F32),
        grid=(nt,),
        in_specs=[smem_blk, smem_blk, next_blk, next_blk,
                  pl.BlockSpec((tc, LANES), lambda i: (i, 0)),
                  pl.BlockSpec((tc, d), lambda i: (i, 0)),
                  pl.BlockSpec((1, 1, d), lambda i: (mod_row(i // per) * 6 + 5, 0, 0)),
                  pl.BlockSpec((1, d), lambda i: (0, 0)),
                  pl.BlockSpec((1, d), lambda i: (0, 0)),
                  pl.BlockSpec(memory_space=pl.ANY)],
        out_specs=pl.BlockSpec((tc, d), lambda i: (i, 0)),
        scratch_shapes=[pltpu.VMEM((2, 2, tc, d), F32), pltpu.SemaphoreType.DMA((2,))],
        compiler_params=_cparams(("arbitrary",)),
        name="moe_combine",
    )(d0r, d1r, d0r, d1r, rt, x2, mod6, ln_g.reshape(1, d), ln_b.reshape(1, d), ys)


def _moe_layer(xs, mod6, mod_row, w_router, w_gate, w_up, w_down, ln_g, ln_b, alpha):
    bsz, s, d = xs.shape
    t = bsz * s
    x2 = xs.reshape(t, d)
    tile = min(MOE_ROW_TILE, t)
    n_blocks = (2 * t) // tile + N_EXPERTS
    u2, rt, tot = _router(x2, mod6, mod_row, s, w_router)
    ri = rt[:, :8].astype(I32)
    tot_i = tot[0].astype(I32)
    xs_sorted, d0, d1, be = _dispatch(tot_i, ri[:, 0], ri[:, 1], ri[:, 4], ri[:, 5], u2, n_blocks, tile)
    ys = _expert_ffn(be, xs_sorted, w_gate, w_up, w_down, tile)
    out = _combine(d0, d1, rt, x2, mod6, mod_row, s, ln_g, ln_b, ys, alpha)
    return out.reshape(bsz, s, d)


def kernel(x, c, ctx, c_ctx, w_ada, b_ada, w_in, conv_w, conv_b, mlstm_gate_b, hgrn_lb,
           ret_decay_logit, head_norm_g, w_branch, w_out, post_ln_g, post_ln_b,
           ffn_w_gate, ffn_w_up, ffn_w_down, moe_w_router, moe_w_gate, moe_w_up, moe_w_down):
    bsz, s_lat, d = x.shape
    s_ctx = ctx.shape[1]
    depth = w_ada.shape[0]
    bw = d // 2
    n_gate = 4 * MLSTM_HEADS
    assert bsz < MOD_ROWS - 1 and s_lat % SCAN_CHUNK == 0 and s_ctx % SCAN_CHUNK == 0
    assert s_lat % GRID_W == 0 and (s_lat <= 1024 or s_lat % 1024 == 0)
    alpha = float((2 * depth) ** 0.25)
    ctx_row = MOD_ROWS // 2
    lat_rows = lambda b: b
    ctx_rows = lambda b: b * 0 + ctx_row

    col = {"mqk": 0, "mv": 2 * bw, "mz": 3 * bw, "hq": 4 * bw, "hi": 5 * bw, "hf": 6 * bw,
           "hg": 8 * bw, "rq": 9 * bw, "rk": 9 * bw + bw // 2, "rv": 10 * bw, "rg": 11 * bw,
           "merge": 12 * bw}
    g0 = 4 * bw

    cv = jnp.zeros((MOD_ROWS, d), F32).at[:bsz].set(c).at[ctx_row].set(c_ctx)
    mod = _ada(cv, w_ada, b_ada)

    h = ctx
    for l in range(depth):
        need_ctx = l < depth - 1
        mod6 = mod[l].reshape(MOD_ROWS * 6, 1, d)
        w_l = w_in[l]
        w_main = jnp.concatenate([w_l[:, :g0], w_l[:, g0 + n_gate:]], axis=1).astype(BF16)
        w_gate = jnp.pad(w_l[:, g0:g0 + n_gate], ((0, 0), (0, LANES - n_gate))).astype(BF16)

        y_x, g_x = _inproj(x, mod6, lat_rows, w_main, w_gate)
        y_h, g_h = _inproj(h, mod6, ctx_rows, w_main, w_gate)
        k_scale = float(bw // MLSTM_HEADS) ** -0.5
        qk = (_conv_silu(y_x, conv_w[l], conv_b[l], True, k_scale),
              _conv_silu(y_h, conv_w[l], conv_b[l], False, k_scale))
        y = (y_x, y_h)
        br_m = _mlstm(qk, y, (g_x, g_h), mlstm_gate_b[l], head_norm_g[l, 0], col["mv"], col["mz"])
        br_h = _hgrn(y, hgrn_lb, l, head_norm_g[l, 1], col["hq"], col["hi"], col["hf"], col["hg"], bw)
        br_r = _retention(y, ret_decay_logit[l], head_norm_g[l, 2],
                          col["rq"], col["rk"], col["rv"], col["rg"],
                          bw // RET_HEADS // 2, bw // RET_HEADS)
        w_br = w_branch[l].astype(BF16)
        w_o = w_out[l].astype(BF16)
        mixed = _merge_proj(br_m[0], br_h[0], br_r[0], y_x, w_br, col["merge"])
        x = _down(mixed, w_o, x, mod6, lat_rows, 2, post_ln_g[l, 0], post_ln_b[l, 0], alpha)
        if need_ctx:
            mixed_h = _merge_proj(br_m[1], br_h[1], br_r[1], y_h, w_br, col["merge"])
            h = _down(mixed_h, w_o, h, mod6, ctx_rows, 2, post_ln_g[l, 0], post_ln_b[l, 0], alpha)

        if l % 2 == 0:
            wg = ffn_w_gate[l // 2].astype(BF16)
            wu = ffn_w_up[l // 2].astype(BF16)
            wd = ffn_w_down[l // 2].astype(BF16)
            x = _down(_ffn_up(x, mod6, lat_rows, wg, wu), wd, x, mod6, lat_rows, 5,
                      post_ln_g[l, 1], post_ln_b[l, 1], alpha)
            if need_ctx:
                h = _down(_ffn_up(h, mod6, ctx_rows, wg, wu), wd, h, mod6, ctx_rows, 5,
                          post_ln_g[l, 1], post_ln_b[l, 1], alpha)
        else:
            e = l // 2
            wg = moe_w_gate[e].astype(BF16)
            wu = moe_w_up[e].astype(BF16)
            wd = moe_w_down[e].astype(BF16)
            x = _moe_layer(x, mod6, lat_rows, moe_w_router[e], wg, wu, wd,
                           post_ln_g[l, 1], post_ln_b[l, 1], alpha)
            if need_ctx:
                h = _moe_layer(h, mod6, ctx_rows, moe_w_router[e], wg, wu, wd,
                               post_ln_g[l, 1], post_ln_b[l, 1], alpha)
    return x
```

```python
import functools

import numpy as np
import jax
import jax.numpy as jnp
from jax import lax
from jax.experimental import pallas as pl
from jax.experimental.pallas import tpu as pltpu

F32 = jnp.float32
BF16 = jnp.bfloat16
I32 = jnp.int32

EPS = 1e-6
NEG = -1e30
TINY = 1e-30

LANES = 128
SUBLANES = 8
SCAN_CHUNK = 256
VMEM_LIMIT_BYTES = 56 * 1024 * 1024
GRID_W = 64
MLSTM_HEADS = 4
HGRN_HEADS = 8
HGRN_HEADS_PER_STEP = 8
HGRN_CHUNK = 128
RET_HEADS = 4
N_EXPERTS = 8
MOD_ROWS = 16


def _cparams(sem):
    return pltpu.CompilerParams(dimension_semantics=sem, vmem_limit_bytes=VMEM_LIMIT_BYTES)


def _dot(a, b):
    return jnp.dot(a, b, preferred_element_type=F32)


def _dot_nt(a, b):
    return lax.dot_general(a, b, (((1,), (1,)), ((), ())), preferred_element_type=F32)


def _dot_tn(a, b):
    return lax.dot_general(a, b, (((0,), (0,)), ((), ())), preferred_element_type=F32)


def _split_bf16(a):
    hi = a.astype(BF16)
    lo = (a - hi.astype(F32)).astype(BF16)
    return hi, lo


def _silu(a):
    return a * jax.nn.sigmoid(a)


def _ln_rows(a):
    mu = jnp.mean(a, axis=-1, keepdims=True)
    ac = a - mu
    var = jnp.mean(ac * ac, axis=-1, keepdims=True)
    return ac * lax.rsqrt(var + EPS)


def _ada_kernel(cv_ref, w_ref, b_ref, o_ref):
    a = _silu(cv_ref[...])
    a_hi, a_lo = _split_bf16(a)
    w_hi, w_lo = _split_bf16(w_ref[0])
    acc = _dot(a_hi, w_hi) + _dot(a_lo, w_hi) + _dot(a_hi, w_lo)
    o_ref[0] = acc + b_ref[0]


def _ada(cv, w_ada, b_ada):
    n_layers, d, n = w_ada.shape
    tn = 1024
    return pl.pallas_call(
        _ada_kernel,
        out_shape=jax.ShapeDtypeStruct((n_layers, MOD_ROWS, n), F32),
        grid=(n_layers, n // tn),
        in_specs=[pl.BlockSpec((MOD_ROWS, d), lambda l, j: (0, 0)),
                  pl.BlockSpec((1, d, tn), lambda l, j: (l, 0, j)),
                  pl.BlockSpec((1, 1, tn), lambda l, j: (l, 0, j))],
        out_specs=pl.BlockSpec((1, MOD_ROWS, tn), lambda l, j: (l, 0, j)),
        compiler_params=_cparams(("parallel", "parallel")),
        name="ada",
    )(cv, w_ada, b_ada.reshape(n_layers, 1, n))


def _inproj_kernel(x_ref, sh_ref, sc_ref, w_ref, wg_ref, y_ref, g_ref, u_ref):
    j = pl.program_id(2)

    @pl.when(j == 0)
    def _():
        u = _ln_rows(x_ref[0]) * (1.0 + sc_ref[0]) + sh_ref[0]
        ub = u.astype(BF16)
        u_ref[...] = ub
        g_ref[0] = _dot(ub, wg_ref[...])

    y_ref[0] = _dot(u_ref[...], w_ref[...]).astype(y_ref.dtype)


def _inproj(xs, mod6, mod_row, w_main, w_gate):
    bsz, s, d = xs.shape
    n = w_main.shape[1]
    tm = min(1024, s)
    tn = 1024
    row = mod_row
    return pl.pallas_call(
        _inproj_kernel,
        out_shape=(jax.ShapeDtypeStruct((bsz, s, n), BF16),
                   jax.ShapeDtypeStruct((bsz, s, LANES), F32)),
        grid=(bsz, s // tm, n // tn),
        in_specs=[pl.BlockSpec((1, tm, d), lambda b, i, j: (b, i, 0)),
                  pl.BlockSpec((1, 1, d), lambda b, i, j: (row(b) * 6 + 0, 0, 0)),
                  pl.BlockSpec((1, 1, d), lambda b, i, j: (row(b) * 6 + 1, 0, 0)),
                  pl.BlockSpec((d, tn), lambda b, i, j: (0, j)),
                  pl.BlockSpec((d, LANES), lambda b, i, j: (0, 0))],
        out_specs=(pl.BlockSpec((1, tm, tn), lambda b, i, j: (b, i, j)),
                   pl.BlockSpec((1, tm, LANES), lambda b, i, j: (b, i, 0))),
        scratch_shapes=[pltpu.VMEM((tm, d), BF16)],
        compiler_params=_cparams(("parallel", "parallel", "arbitrary")),
        name="inproj",
    )(xs, mod6, mod6, w_main, w_gate)


_CONV_PAD = 72


def _conv_kernel(a_ref, w_ref, b_ref, o_ref, pad_ref, *, seq, on_grid, k_scale, k_from):
    ct = a_ref.shape[2]
    zeros = jnp.zeros((_CONV_PAD, ct), F32)
    pad_ref[pl.ds(0, _CONV_PAD), :] = zeros
    pad_ref[pl.ds(_CONV_PAD + seq, _CONV_PAD), :] = zeros
    pad_ref[pl.ds(_CONV_PAD, seq), :] = a_ref[0].astype(F32)
    w = w_ref[...]
    bias = b_ref[...]
    scale = jnp.where(pl.program_id(1) >= k_from, k_scale, 1.0).astype(F32)

    rc = min(512, seq)
    for r0 in range(0, seq, rc):
        col = lax.broadcasted_iota(I32, (rc, 1), 0) % GRID_W
        acc = jnp.zeros((rc, ct), F32)
        for dr in range(3) if on_grid else (1,):
            for dc in range(3):
                off = (dr - 1) * GRID_W + (dc - 1)
                tap = pad_ref[pl.ds(_CONV_PAD + r0 + off, rc), :]
                if on_grid and dc == 0:
                    tap = jnp.where(col == 0, 0.0, tap)
                elif on_grid and dc == 2:
                    tap = jnp.where(col == GRID_W - 1, 0.0, tap)
                acc = acc + tap * w[dr * 3 + dc:dr * 3 + dc + 1, :]
        o_ref[0, pl.ds(r0, rc), :] = (_silu(acc + bias) * scale).astype(o_ref.dtype)


def _conv_silu(y, conv_w, conv_b, on_grid, k_scale):
    bsz, seq, _ = y.shape
    ch = conv_w.shape[-1]
    ct = 256
    kern = functools.partial(_conv_kernel, seq=seq, on_grid=on_grid, k_scale=k_scale,
                             k_from=(ch // 2) // ct)
    return pl.pallas_call(
        kern,
        out_shape=jax.ShapeDtypeStruct((bsz, seq, ch), BF16),
        grid=(bsz, ch // ct),
        in_specs=[pl.BlockSpec((1, seq, ct), lambda b, j: (b, 0, j)),
                  pl.BlockSpec((9, ct), lambda b, j: (0, j)),
                  pl.BlockSpec((1, ct), lambda b, j: (0, j))],
        out_specs=pl.BlockSpec((1, seq, ct), lambda b, j: (b, 0, j)),
        scratch_shapes=[pltpu.VMEM((seq + 2 * _CONV_PAD, ct), F32)],
        compiler_params=_cparams(("parallel", "parallel")),
        name="conv_silu",
    )(y, conv_w.reshape(9, ch), conv_b.reshape(1, ch))


def _sched(s, n_lat, n_ctx):
    nc = n_lat + n_ctx
    d = s // nc
    p = s - d * nc
    is_ctx = p < n_ctx
    pc = jnp.minimum(p, n_ctx - 1)
    pq = jnp.maximum(p - n_ctx, 0)
    ctx_c = jnp.where(d == 0, pc, n_ctx - 1 - pc)
    lat_c = jnp.where(d == 0, pq, n_lat - 1 - pq)
    return d, p, is_ctx, lat_c, ctx_c


def _scan_specs(n_lat, n_ctx, bp=1, c=SCAN_CHUNK):
    nc = n_lat + n_ctx

    def in_pair(width, col_blk):
        return (pl.BlockSpec((bp, c, width), lambda b, h, s: (b, _sched(s, n_lat, n_ctx)[3], col_blk(h, s))),
                pl.BlockSpec((bp, c, width), lambda b, h, s: (b, _sched(s, n_lat, n_ctx)[4], col_blk(h, s))))

    def out_pair(width, col_blk):
        return (pl.BlockSpec((bp, c, width),
                             lambda b, h, s: (b, _sched(jnp.maximum(s, nc), n_lat, n_ctx)[3], col_blk(h, s))),
                pl.BlockSpec((bp, c, width),
                             lambda b, h, s: (b, _sched(jnp.maximum(s, nc), n_lat, n_ctx)[4], col_blk(h, s))))

    return in_pair, out_pair


def _store_scan_out(is_ctx, val, ol_ref, oc_ref, bi=0):
    @pl.when(is_ctx)
    def _():
        oc_ref[bi] = val.astype(oc_ref.dtype)

    @pl.when(jnp.logical_not(is_ctx))
    def _():
        ol_ref[bi] = val.astype(ol_ref.dtype)


SCAN_BATCH_PER_STEP = 2


def _log_sigmoid(a):
    return jnp.minimum(a, 0.0) - jnp.log1p(jnp.exp(-jnp.abs(a)))


def _prefix_sums(cols, rows, tri_b, nh):
    lane = lax.broadcasted_iota(I32, (1, LANES), 1)
    hi = cols.astype(BF16).astype(F32)
    rhs = jnp.where(lane < nh, hi, pltpu.roll(cols - hi, nh, axis=1)).astype(BF16)
    bc = _dot(tri_b, rhs)
    col = bc + pltpu.roll(bc, LANES - nh, axis=1)
    sub = lax.broadcasted_iota(I32, (SUBLANES, 1), 0)
    rhi = rows.astype(BF16).astype(F32)
    lhs = jnp.where(sub < nh, rhi, pltpu.roll(rows - rhi, nh, axis=0))
    lhs = jnp.concatenate([lhs, jnp.zeros_like(lhs)], axis=0).astype(BF16)
    br = _dot_nt(lhs, tri_b)[0:SUBLANES]
    row = br + pltpu.roll(br, SUBLANES - nh, axis=0)
    return col, row


def _mlstm_kernel(ql_ref, qc_ref, kl_ref, kc_ref, vl_ref, vc_ref, zl_ref, zc_ref, gl_ref, gc_ref,
                  gb_ref, ng_ref, ol_ref, oc_ref, c_ref, n_ref, m_ref, hbuf_ref, *, n_lat, n_ctx):
    d, p, is_ctx, lat_c, ctx_c = _sched(pl.program_id(2), n_lat, n_ctx)

    @pl.when(p == 0)
    def _():
        c_ref[...] = jnp.zeros_like(c_ref)
        n_ref[...] = jnp.zeros_like(n_ref)
        m_ref[...] = jnp.full_like(m_ref, NEG)

    for bi in range(c_ref.shape[0]):
        _mlstm_step(bi, d, is_ctx, lat_c, ctx_c, ql_ref, qc_ref, kl_ref, kc_ref, vl_ref, vc_ref,
                    zl_ref, zc_ref, gl_ref, gc_ref, gb_ref, ng_ref, ol_ref, oc_ref,
                    c_ref.at[bi], n_ref.at[bi], m_ref.at[bi], hbuf_ref.at[bi], n_lat)


def _mlstm_step(bi, d, is_ctx, lat_c, ctx_c, ql_ref, qc_ref, kl_ref, kc_ref, vl_ref, vc_ref,
                zl_ref, zc_ref, gl_ref, gc_ref, gb_ref, ng_ref, ol_ref, oc_ref,
                c_ref, n_ref, m_ref, hbuf_ref, n_lat):
    nh = MLSTM_HEADS
    chunk = jnp.where(is_ctx, n_lat + ctx_c, lat_c)
    pick = lambda a_l, a_c: jnp.where(is_ctx, a_c[bi], a_l[bi])
    c = SCAN_CHUNK
    dh = c_ref.shape[1]

    gates = pick(gl_ref, gc_ref) + gb_ref[...]
    gates = jnp.where(d == 0, gates, pltpu.roll(gates, LANES - 2 * nh, axis=1))
    lane = lax.broadcasted_iota(I32, (1, LANES), 1)
    r32 = jnp.where(lane < nh, gates, jnp.where(lane < 2 * nh, _log_sigmoid(gates), 0.0))
    rt = r32.T[0:SUBLANES]
    lf_cols = jnp.where(lane < nh, pltpu.roll(r32, LANES - nh, axis=1), 0.0)
    sub = lax.broadcasted_iota(I32, (SUBLANES, 1), 0)
    lf_rows = jnp.where(sub < nh, pltpu.roll(rt, SUBLANES - nh, axis=0), 0.0)

    t_i = lax.broadcasted_iota(I32, (c, c), 0)
    s_i = lax.broadcasted_iota(I32, (c, c), 1)
    tri = (t_i - s_i) * (1 - 2 * d) >= 0
    b_cols, b_rows = _prefix_sums(lf_cols, lf_rows, tri.astype(BF16), nh)
    totals = jnp.sum(lf_cols, axis=0, keepdims=True)

    q_all = pick(ql_ref, qc_ref)
    k_all = pick(kl_ref, kc_ref)
    v_all = pick(vl_ref, vc_ref)
    outs = []
    for j in range(nh):
        li = r32[:, j:j + 1]
        li_row = rt[j:j + 1, :]
        b_col = b_cols[:, j:j + 1]
        b_row = b_rows[j:j + 1, :]
        total = totals[:, j:j + 1]
        q = q_all[:, j * dh:(j + 1) * dh]
        k = k_all[:, j * dh:(j + 1) * dh]
        vb = v_all[:, j * dh:(j + 1) * dh].astype(BF16)

        m_prev = m_ref[j, 0:1, 0:1]
        d_mat = jnp.where(tri, b_col - b_row + li_row, NEG)
        a_inter = b_col + m_prev
        m_t = jnp.maximum(a_inter, jnp.max(d_mat, axis=1, keepdims=True))
        w_inter = jnp.exp(a_inter - m_t)
        s_mat = _dot_nt(q, k) * jnp.exp(d_mat - m_t)
        num = w_inter * _dot(q, c_ref[j].astype(BF16)) + _dot(s_mat.astype(BF16), vb)
        den = (w_inter * jnp.sum(q.astype(F32) * n_ref[j, 0:1, :], axis=1, keepdims=True)
               + jnp.sum(s_mat, axis=1, keepdims=True))
        outs.append(num / jnp.maximum(jnp.abs(den), jnp.exp(-m_t)))

        g_col = total - b_col + li
        m_new = jnp.maximum(total + m_prev, jnp.max(g_col, axis=0, keepdims=True))
        decay = jnp.exp(total + m_prev - m_new)
        kw = k.astype(F32) * jnp.exp(g_col - m_new)
        c_ref[j] = decay * c_ref[j] + _dot_tn(kw.astype(BF16), vb)
        n_new = decay * n_ref[j, 0:1, :] + jnp.sum(kw, axis=0, keepdims=True)
        n_ref[j] = jnp.broadcast_to(n_new, n_ref.shape[1:])
        m_ref[j] = jnp.broadcast_to(m_new, m_ref.shape[1:])

    @pl.when(d == 0)
    def _():
        for j in range(nh):
            hbuf_ref[chunk, :, j * dh:(j + 1) * dh] = outs[j].astype(hbuf_ref.dtype)

    @pl.when(d == 1)
    def _():
        z = pick(zl_ref, zc_ref).astype(F32)
        fin = []
        for j in range(nh):
            sl = slice(j * dh, (j + 1) * dh)
            hn = _ln_rows(hbuf_ref[chunk, :, sl].astype(F32) + outs[j])
            fin.append(hn * ng_ref[:, sl] * _silu(z[:, sl]))
        _store_scan_out(is_ctx, jnp.concatenate(fin, axis=1), ol_ref, oc_ref, bi)


def _mlstm(qk, y, g, gate_b, norm_g, v_col, z_col):
    bsz, s_lat, w2 = qk[0].shape
    s_ctx = qk[1].shape[1]
    width = w2 // 2
    dh = width // MLSTM_HEADS
    c = SCAN_CHUNK
    n_lat, n_ctx = s_lat // c, s_ctx // c
    bp = 1
    in_pair, out_pair = _scan_specs(n_lat, n_ctx, bp)
    kern = functools.partial(_mlstm_kernel, n_lat=n_lat, n_ctx=n_ctx)
    gb = jnp.zeros((1, LANES), F32).at[0, :gate_b.shape[0]].set(gate_b)
    return pl.pallas_call(
        kern,
        out_shape=(jax.ShapeDtypeStruct((bsz, s_lat, width), BF16),
                   jax.ShapeDtypeStruct((bsz, s_ctx, width), BF16)),
        grid=(bsz // bp, 1, 2 * (n_lat + n_ctx)),
        in_specs=[*in_pair(width, lambda h, s: 0),
                  *in_pair(width, lambda h, s: 1),
                  *in_pair(width, lambda h, s: v_col // width),
                  *out_pair(width, lambda h, s: z_col // width),
                  *in_pair(LANES, lambda h, s: 0),
                  pl.BlockSpec((1, LANES), lambda b, h, s: (0, 0)),
                  pl.BlockSpec((1, width), lambda b, h, s: (0, 0))],
        out_specs=out_pair(width, lambda h, s: 0),
        scratch_shapes=[pltpu.VMEM((bp, MLSTM_HEADS, dh, dh), F32),
                        pltpu.VMEM((bp, MLSTM_HEADS, SUBLANES, dh), F32),
                        pltpu.VMEM((bp, MLSTM_HEADS, SUBLANES, LANES), F32),
                        pltpu.VMEM((bp, n_lat + n_ctx, c, width), BF16)],
        compiler_params=_cparams(("parallel", "parallel", "arbitrary")),
        name="mlstm",
    )(qk[0], qk[1], qk[0], qk[1], y[0], y[1], y[0], y[1], g[0], g[1], gb, norm_g.reshape(1, width))


def _ret_kernel(ql_ref, qc_ref, kl_ref, kc_ref, vl_ref, vc_ref, zl_ref, zc_ref, dl_ref, ng_ref,
                ol_ref, oc_ref, s_ref, dmat_ref, hbuf_ref, *, n_lat, n_ctx, q_scale):
    nh = RET_HEADS
    d, p, is_ctx, lat_c, ctx_c = _sched(pl.program_id(2), n_lat, n_ctx)
    chunk = jnp.where(is_ctx, n_lat + ctx_c, lat_c)
    c = SCAN_CHUNK
    bp, dk, dv = s_ref.shape[0], s_ref.shape[2], s_ref.shape[3]

    lane = lax.broadcasted_iota(I32, (1, LANES), 1)
    lgs = _log_sigmoid(dl_ref[...])
    lg = [jnp.sum(jnp.where(lane == d * nh + j, lgs, 0.0), axis=1, keepdims=True) for j in range(nh)]

    @pl.when(p == 0)
    def _():
        s_ref[...] = jnp.zeros_like(s_ref)
        t_i = lax.broadcasted_iota(I32, (c, c), 0)
        s_i = lax.broadcasted_iota(I32, (c, c), 1)
        dist = (t_i - s_i) * (1 - 2 * d)
        distf = jnp.maximum(dist, 0).astype(F32)
        for j in range(nh):
            dmat_ref[j] = jnp.where(dist >= 0, jnp.exp(lg[j] * distf), 0.0)

    t_c = lax.broadcasted_iota(I32, (c, 1), 0)
    pos = jnp.where(d == 0, t_c + 1, c - t_c).astype(F32)
    inter_w = [jnp.exp(lg[j] * pos) for j in range(nh)]
    state_w = [jnp.exp(lg[j] * (c - pos)) for j in range(nh)]
    for bi in range(bp):
        pick = lambda a_l, a_c: jnp.where(is_ctx, a_c[bi], a_l[bi])
        q_all = pick(ql_ref, qc_ref)
        k_all = pick(kl_ref, kc_ref)
        v_all = pick(vl_ref, vc_ref)
        outs = []
        for j in range(nh):
            qb = (q_all[:, j * dk:(j + 1) * dk].astype(F32) * q_scale).astype(BF16)
            kf = k_all[:, j * dk:(j + 1) * dk].astype(F32)
            vb = v_all[:, j * dv:(j + 1) * dv].astype(BF16)
            a = _dot_nt(qb, kf.astype(BF16)) * dmat_ref[j]
            outs.append(inter_w[j] * _dot(qb, s_ref[bi, j].astype(BF16)) + _dot(a.astype(BF16), vb))
            s_ref[bi, j] = (jnp.exp(lg[j] * c) * s_ref[bi, j]
                            + _dot_tn((kf * state_w[j]).astype(BF16), vb))

        @pl.when(d == 0)
        def _():
            for j in range(nh):
                hbuf_ref[bi, chunk, :, j * dv:(j + 1) * dv] = outs[j].astype(hbuf_ref.dtype)

        @pl.when(d == 1)
        def _():
            z = pick(zl_ref, zc_ref).astype(F32)
            fin = []
            for j in range(nh):
                sl = slice(j * dv, (j + 1) * dv)
                hn = _ln_rows(hbuf_ref[bi, chunk, :, sl].astype(F32) + outs[j])
                fin.append(hn * ng_ref[:, sl] * _silu(z[:, sl]))
            _store_scan_out(is_ctx, jnp.concatenate(fin, axis=1), ol_ref, oc_ref, bi)


def _retention(y, decay_logit, norm_g, q_col, k_col, v_col, z_col, dk, dv):
    bsz, s_lat, _ = y[0].shape
    s_ctx = y[1].shape[1]
    c = SCAN_CHUNK
    n_lat, n_ctx = s_lat // c, s_ctx // c
    bp = SCAN_BATCH_PER_STEP if bsz % SCAN_BATCH_PER_STEP == 0 else 1
    in_pair, out_pair = _scan_specs(n_lat, n_ctx, bp)
    kern = functools.partial(_ret_kernel, n_lat=n_lat, n_ctx=n_ctx, q_scale=float(dk) ** -0.5)
    wk, wv = RET_HEADS * dk, RET_HEADS * dv
    dl = jnp.zeros((1, LANES), F32).at[0, :2 * RET_HEADS].set(decay_logit.reshape(-1))
    return pl.pallas_call(
        kern,
        out_shape=(jax.ShapeDtypeStruct((bsz, s_lat, RET_HEADS * dv), BF16),
                   jax.ShapeDtypeStruct((bsz, s_ctx, RET_HEADS * dv), BF16)),
        grid=(bsz // bp, 1, 2 * (n_lat + n_ctx)),
        in_specs=[*in_pair(wk, lambda h, s: q_col // wk),
                  *in_pair(wk, lambda h, s: k_col // wk),
                  *in_pair(wv, lambda h, s: v_col // wv),
                  *out_pair(wv, lambda h, s: z_col // wv),
                  pl.BlockSpec((1, LANES), lambda b, h, s: (0, 0)),
                  pl.BlockSpec((1, wv), lambda b, h, s: (0, 0))],
        out_specs=out_pair(wv, lambda h, s: 0),
        scratch_shapes=[pltpu.VMEM((bp, RET_HEADS, dk, dv), F32),
                        pltpu.VMEM((RET_HEADS, c, c), F32),
                        pltpu.VMEM((bp, n_lat + n_ctx, c, wv), BF16)],
        compiler_params=_cparams(("parallel", "parallel", "arbitrary")),
        name="retention",
    )(y[0], y[1], y[0], y[1], y[0], y[1], y[0], y[1], dl, norm_g.reshape(1, RET_HEADS * dv))


def _hgrn_tables(c):
    nl = int(np.log2(c))
    t = np.arange(c)[:, None]
    u = np.arange(c)[None, :]
    x = (t ^ u).astype(np.int64)
    lvl = np.where(x == 0, nl, np.floor(np.log2(np.maximum(x, 1))).astype(np.int64))
    lvl2 = np.stack([np.where(u <= t, lvl, -1), np.where(u >= t, lvl, -1)])
    return (jnp.asarray((u <= t).astype(np.float32), BF16), jnp.asarray(lvl2, I32), nl)


def _block_ref_rows(b, lev):
    c, w = b.shape
    n = 1 << lev
    if 2 * n >= SUBLANES:
        b3 = b.reshape(c // (2 * n), 2 * n, w)
        return jnp.broadcast_to(b3[:, n - 1:n, :], b3.shape).reshape(c, w)
    t = lax.broadcasted_iota(I32, (c, 1), 0) & (2 * n - 1)
    out = b
    for k in range(2 * n):
        if k != n - 1:
            out = jnp.where(t == k, pltpu.roll(b, (k - (n - 1)) % c, axis=0), out)
    return out


def _hgrn_kernel(ql_ref, qc_ref, vl_ref, vc_ref, fl_ref, fc_ref, zl_ref, zc_ref, lb_ref, ng_ref,
                 tab_ref, lvl_ref, ol_ref, oc_ref, st_ref, hbuf_ref, *, n_lat, n_ctx, layer, nl):
    d, p, is_ctx, lat_c, ctx_c = _sched(pl.program_id(2), n_lat, n_ctx)
    chunk = jnp.where(is_ctx, n_lat + ctx_c, lat_c)
    pick = lambda a_l, a_c: jnp.where(is_ctx, a_c[0], a_l[0])
    c = tab_ref.shape[0]

    @pl.when(p == 0)
    def _():
        st_ref[...] = jnp.zeros_like(st_ref)

    lbp = lb_ref[0]
    sm = jnp.exp(lbp - jnp.max(lbp, axis=0, keepdims=True))
    sm = sm / jnp.sum(sm, axis=0, keepdims=True)
    lb = jnp.zeros((1, lbp.shape[1]), F32)
    for i in range(1, layer + 1):
        lb = lb + sm[i:i + 1, :]

    ft = pick(fl_ref, fc_ref).astype(F32)
    kk = (1.0 - lb) * jax.nn.sigmoid(-ft)
    lf = jnp.log(jnp.maximum(lb + (1.0 - lb) * jax.nn.sigmoid(ft), TINY))
    q = _silu(pick(ql_ref, qc_ref).astype(F32))
    v_all = pick(vl_ref, vc_ref)
    w = lf.shape[1]
    dk = st_ref.shape[1]
    nh = w // dk
    lf_hi = lf.astype(BF16)
    r1 = lf - lf_hi.astype(F32)
    lf_mid = r1.astype(BF16)
    lf_lo = (r1 - lf_mid.astype(F32)).astype(BF16)
    ps = _dot(tab_ref[...], jnp.concatenate([lf_hi, lf_mid, lf_lo], axis=1))
    b = ps[:, :w] + ps[:, w:2 * w] + ps[:, 2 * w:]
    total = jnp.sum(lf, axis=0, keepdims=True)
    bx = b - jnp.where(d == 0, 0.0, 1.0) * lf

    def seg_exp(idx):
        if idx < nl:
            return jnp.exp(-jnp.abs(bx - _block_ref_rows(b, idx)))
        query_side = (idx == nl)
        from_start = jnp.where(d == 0, 1.0, 0.0) if query_side else jnp.where(d == 0, 0.0, 1.0)
        return jnp.exp(from_start * bx + (1.0 - from_start) * (total - bx))

    t_c = lax.broadcasted_iota(I32, (c, 1), 0)
    lvl = lvl_ref[0]
    hs = [slice(j * dk, (j + 1) * dk) for j in range(nh)]
    qb = q.astype(BF16)
    kb = kk.astype(BF16)
    accs = [jnp.where(lvl == nl, _dot_nt(qb[:, sl], kb[:, sl]), 0.0) for sl in hs]
    for lev in range(nl):
        is_q = ((t_c >> lev) & 1) != d
        z = (jnp.where(is_q, q, kk) * seg_exp(lev)).astype(BF16)
        hit = lvl == lev
        accs = [jnp.where(hit, _dot_nt(z[:, sl], z[:, sl]), a) for sl, a in zip(hs, accs)]

    qg = (q * seg_exp(nl)).astype(BF16)
    kg = (kk * seg_exp(nl + 1)).astype(BF16)
    decay = jnp.exp(jnp.sum(lf, axis=0, keepdims=True))
    outs = []
    for j, sl in enumerate(hs):
        st = st_ref[j]
        vb = v_all[:, sl].astype(BF16)
        outs.append(_dot(accs[j].astype(BF16), vb) + _dot_nt(qg[:, sl], st.astype(BF16)))
        st_ref[j] = st * decay[:, sl] + _dot_tn(vb, kg[:, sl])

    @pl.when(d == 0)
    def _():
        for j, sl in enumerate(hs):
            hbuf_ref[chunk, :, sl] = outs[j]

    @pl.when(d == 1)
    def _():
        z = pick(zl_ref, zc_ref).astype(F32)
        fin = []
        for j, sl in enumerate(hs):
            tot = hbuf_ref[chunk, :, sl] + outs[j]
            hn = tot * lax.rsqrt(jnp.mean(tot * tot, axis=-1, keepdims=True) + EPS)
            fin.append(hn * ng_ref[:, sl] * _silu(z[:, sl]))
        _store_scan_out(is_ctx, jnp.concatenate(fin, axis=1), ol_ref, oc_ref)


def _hgrn(y, hgrn_lb, layer, norm_g, q_col, v_col, f_col, z_col, width):
    bsz, s_lat, _ = y[0].shape
    s_ctx = y[1].shape[1]
    dk = width // HGRN_HEADS
    depth = hgrn_lb.shape[1]
    c = HGRN_CHUNK
    n_lat, n_ctx = s_lat // c, s_ctx // c
    nc = n_lat + n_ctx
    tabs, lvl, nl = _hgrn_tables(c)
    in_pair, out_pair = _scan_specs(n_lat, n_ctx, 1, c)
    kern = functools.partial(_hgrn_kernel, n_lat=n_lat, n_ctx=n_ctx, layer=layer, nl=nl)
    hp = HGRN_HEADS_PER_STEP
    wg = hp * dk
    per_dir = width // wg
    return pl.pallas_call(
        kern,
        out_shape=(jax.ShapeDtypeStruct((bsz, s_lat, width), BF16),
                   jax.ShapeDtypeStruct((bsz, s_ctx, width), BF16)),
        grid=(bsz, HGRN_HEADS // hp, 2 * nc),
        in_specs=[*in_pair(wg, lambda h, s: q_col // wg + h),
                  *in_pair(wg, lambda h, s: v_col // wg + h),
                  *in_pair(wg, lambda h, s: f_col // wg + (s // nc) * per_dir + h),
                  *out_pair(wg, lambda h, s: z_col // wg + h),
                  pl.BlockSpec((1, depth, wg), lambda b, h, s: (s // nc, 0, h)),
                  pl.BlockSpec((1, wg), lambda b, h, s: (0, h)),
                  pl.BlockSpec((c, c), lambda b, h, s: (0, 0)),
                  pl.BlockSpec((1, c, c), lambda b, h, s: (s // nc, 0, 0))],
        out_specs=out_pair(wg, lambda h, s: h),
        scratch_shapes=[pltpu.VMEM((hp, dk, dk), F32),
                        pltpu.VMEM((nc, c, wg), F32)],
        compiler_params=_cparams(("parallel", "parallel", "arbitrary")),
        name="hgrn2",
    )(y[0], y[1], y[0], y[1], y[0], y[1], y[0], y[1], hgrn_lb, norm_g.reshape(1, width), tabs, lvl)


def _merge_kernel(bm_ref, bh_ref, br_ref, g0_ref, g1_ref, g2_ref, w_ref, o_ref):
    acc = jax.nn.sigmoid(g0_ref[0].astype(F32)) * _dot(bm_ref[0], w_ref[0])
    acc = acc + jax.nn.sigmoid(g1_ref[0].astype(F32)) * _dot(bh_ref[0], w_ref[1])
    acc = acc + jax.nn.sigmoid(g2_ref[0].astype(F32)) * _dot(br_ref[0], w_ref[2])
    o_ref[0] = acc.astype(o_ref.dtype)


def _merge_proj(br_m, br_h, br_r, y, w_br, merge_col):
    bsz, s, width = br_m.shape
    d = w_br.shape[2]
    tm = min(1024, s)
    tn = 512
    gcol = merge_col // tn
    nj = d // tn
    bspec = pl.BlockSpec((1, tm, width), lambda b, i, j: (b, i, 0))

    def gspec(n):
        return pl.BlockSpec((1, tm, tn), lambda b, i, j: (b, i, gcol + n * nj + j))

    return pl.pallas_call(
        _merge_kernel,
        out_shape=jax.ShapeDtypeStruct((bsz, s, d), BF16),
        grid=(bsz, s // tm, nj),
        in_specs=[bspec, bspec, bspec, gspec(0), gspec(1), gspec(2),
                  pl.BlockSpec((3, width, tn), lambda b, i, j: (0, 0, j))],
        out_specs=pl.BlockSpec((1, tm, tn), lambda b, i, j: (b, i, j)),
        compiler_params=_cparams(("parallel", "parallel", "arbitrary")),
        name="merge_proj",
    )(br_m, br_h, br_r, y, y, y, w_br)


def _down_kernel(a_ref, w_ref, x_ref, gate_ref, g_ref, b_ref, o_ref, *, alpha, nk):
    k = pl.program_id(2)

    @pl.when(k == 0)
    def _():
        o_ref[...] = jnp.zeros_like(o_ref)

    o_ref[0] += _dot(a_ref[0], w_ref[...])

    @pl.when(k == nk - 1)
    def _():
        z = alpha * x_ref[0] + gate_ref[0] * o_ref[0]
        o_ref[0] = _ln_rows(z) * g_ref[...] + b_ref[...]


def _down(a, w, xs, mod6, mod_row, which, ln_g, ln_b, alpha):
    bsz, s, d = xs.shape
    kdim = w.shape[0]
    tk = next(t for t in (2048, 1024, 512, kdim) if kdim % t == 0)
    nk = kdim // tk
    tm = min(1024, s)
    row = mod_row
    kern = functools.partial(_down_kernel, alpha=alpha, nk=nk)
    return pl.pallas_call(
        kern,
        out_shape=jax.ShapeDtypeStruct((bsz, s, d), F32),
        grid=(bsz, s // tm, nk),
        in_specs=[pl.BlockSpec((1, tm, tk), lambda b, i, k: (b, i, k)),
                  pl.BlockSpec((tk, d), lambda b, i, k: (k, 0)),
                  pl.BlockSpec((1, tm, d), lambda b, i, k: (b, i, 0)),
                  pl.BlockSpec((1, 1, d), lambda b, i, k: (row(b) * 6 + which, 0, 0)),
                  pl.BlockSpec((1, d), lambda b, i, k: (0, 0)),
                  pl.BlockSpec((1, d), lambda b, i, k: (0, 0))],
        out_specs=pl.BlockSpec((1, tm, d), lambda b, i, k: (b, i, 0)),
        compiler_params=_cparams(("parallel", "parallel", "arbitrary")),
        name="down_postnorm",
    )(a, w, xs, mod6, ln_g.reshape(1, d), ln_b.reshape(1, d))


def _ffn_up_kernel(x_ref, sh_ref, sc_ref, wg_ref, wu_ref, h_ref, u_ref):
    @pl.when(pl.program_id(2) == 0)
    def _():
        u = _ln_rows(x_ref[0]) * (1.0 + sc_ref[0]) + sh_ref[0]
        u_ref[...] = u.astype(BF16)

    ub = u_ref[...]
    h_ref[0] = (_silu(_dot(ub, wg_ref[...])) * _dot(ub, wu_ref[...])).astype(h_ref.dtype)


def _ffn_up(xs, mod6, mod_row, w_gate, w_up):
    bsz, s, d = xs.shape
    f = w_gate.shape[1]
    tm = min(1024, s)
    tn = 512
    row = mod_row
    return pl.pallas_call(
        _ffn_up_kernel,
        out_shape=jax.ShapeDtypeStruct((bsz, s, f), BF16),
        grid=(bsz, s // tm, f // tn),
        in_specs=[pl.BlockSpec((1, tm, d), lambda b, i, j: (b, i, 0)),
                  pl.BlockSpec((1, 1, d), lambda b, i, j: (row(b) * 6 + 3, 0, 0)),
                  pl.BlockSpec((1, 1, d), lambda b, i, j: (row(b) * 6 + 4, 0, 0)),
                  pl.BlockSpec((d, tn), lambda b, i, j: (0, j)),
                  pl.BlockSpec((d, tn), lambda b, i, j: (0, j))],
        out_specs=pl.BlockSpec((1, tm, tn), lambda b, i, j: (b, i, j)),
        scratch_shapes=[pltpu.VMEM((tm, d), BF16)],
        compiler_params=_cparams(("parallel", "parallel", "arbitrary")),
        name="ffn_up",
    )(xs, mod6, mod6, w_gate, w_up)


MOE_ROW_TILE = 512


def _router_kernel(x_ref, sh_ref, sc_ref, w_ref, u_ref, rt_ref, tot_ref, carry_ref, *, n_exp):
    i = pl.program_id(0)

    @pl.when(i == 0)
    def _():
        carry_ref[...] = jnp.zeros_like(carry_ref)

    u = _ln_rows(x_ref[...]) * (1.0 + sc_ref[0]) + sh_ref[0]
    u_ref[...] = u
    u_hi, u_lo = _split_bf16(u)
    w_hi, w_lo = _split_bf16(w_ref[...])
    logits = _dot(u_hi, w_hi) + _dot(u_lo, w_hi) + _dot(u_hi, w_lo)
    tr = logits.shape[0]
    lane = lax.broadcasted_iota(I32, (1, LANES), 1)
    ninf = jnp.float32(-jnp.inf)
    lg = jnp.where(lane < n_exp, logits, ninf)
    m1 = jnp.max(lg, axis=1, keepdims=True)
    i1 = jnp.min(jnp.where(lg == m1, lane, LANES), axis=1, keepdims=True)
    lg2 = jnp.where(lane == i1, ninf, lg)
    m2 = jnp.max(lg2, axis=1, keepdims=True)
    i2 = jnp.min(jnp.where(lg2 == m2, lane, LANES), axis=1, keepdims=True)
    e = jnp.exp(m2 - m1)
    w0 = 1.0 / (1.0 + e)
    w1 = e / (1.0 + e)

    oh = ((lane == i1) | (lane == i2)).astype(F32)
    r_i = lax.broadcasted_iota(I32, (tr, tr), 0)
    c_i = lax.broadcasted_iota(I32, (tr, tr), 1)
    cum = _dot((c_i < r_i).astype(BF16), oh.astype(BF16)) + carry_ref[0:1, :]
    r0 = jnp.sum(jnp.where(lane == i1, cum, 0.0), axis=1, keepdims=True)
    r1 = jnp.sum(jnp.where(lane == i2, cum, 0.0), axis=1, keepdims=True)
    new_carry = carry_ref[0:1, :] + jnp.sum(oh, axis=0, keepdims=True)
    carry_ref[...] = jnp.broadcast_to(new_carry, carry_ref.shape)
    tot_ref[...] = jnp.broadcast_to(new_carry, tot_ref.shape)

    rt = jnp.where(lane == 0, i1.astype(F32), 0.0)
    rt = jnp.where(lane == 1, i2.astype(F32), rt)
    rt = jnp.where(lane == 2, w0, rt)
    rt = jnp.where(lane == 3, w1, rt)
    rt = jnp.where(lane == 4, r0, rt)
    rt = jnp.where(lane == 5, r1, rt)
    rt_ref[...] = rt


def _router(x2, mod6, mod_row, seq, w_router):
    t, d = x2.shape
    n_exp = w_router.shape[1]
    tr = min(512, seq)
    wr = jnp.zeros((d, LANES), F32).at[:, :n_exp].set(w_router)
    kern = functools.partial(_router_kernel, n_exp=n_exp)
    per = seq // tr
    return pl.pallas_call(
        kern,
        out_shape=(jax.ShapeDtypeStruct((t, d), F32),
                   jax.ShapeDtypeStruct((t, LANES), F32),
                   jax.ShapeDtypeStruct((SUBLANES, LANES), F32)),
        grid=(t // tr,),
        in_specs=[pl.BlockSpec((tr, d), lambda i: (i, 0)),
                  pl.BlockSpec((1, 1, d), lambda i: (mod_row(i // per) * 6 + 3, 0, 0)),
                  pl.BlockSpec((1, 1, d), lambda i: (mod_row(i // per) * 6 + 4, 0, 0)),
                  pl.BlockSpec((d, LANES), lambda i: (0, 0))],
        out_specs=(pl.BlockSpec((tr, d), lambda i: (i, 0)),
                   pl.BlockSpec((tr, LANES), lambda i: (i, 0)),
                   pl.BlockSpec((SUBLANES, LANES), lambda i: (0, 0))),
        scratch_shapes=[pltpu.VMEM((SUBLANES, LANES), F32)],
        compiler_params=_cparams(("arbitrary",)),
        name="moe_router",
    )(x2, mod6, mod6, wr)


def _dispatch_kernel(tot_ref, e0_ref, e1_ref, r0_ref, r1_ref, u_ref,
                     xs_hbm, d0_ref, d1_ref, be_ref, gs_ref, zero_ref, sem, zsem,
                     *, n_exp, n_blocks, tile, td):
    i = pl.program_id(0)

    @pl.when(i == 0)
    def _():
        start = jnp.int32(0)
        for e in range(n_exp):
            gs_ref[e] = start
            start = start + ((tot_ref[e] + (tile - 1)) // tile) * tile
        gs_ref[n_exp] = start

        def blk(b, carry):
            row = b * tile
            ex = jnp.int32(0)
            for e in range(1, n_exp):
                ex = jnp.where(row >= gs_ref[e], e, ex)
            be_ref[0, b] = ex
            be_ref[1, b] = jnp.where(row < gs_ref[n_exp], 1, 0).astype(I32)
            return carry

        lax.fori_loop(0, n_blocks, blk, 0)

        zero_ref[...] = jnp.zeros_like(zero_ref)
        for e in range(n_exp):
            last = pl.multiple_of(jnp.maximum(gs_ref[e + 1] - tile, 0), tile)
            pltpu.make_async_copy(zero_ref, xs_hbm.at[pl.ds(last, tile)], zsem).start()
        for e in range(n_exp):
            pltpu.make_async_copy(zero_ref, xs_hbm.at[pl.ds(0, tile)], zsem).wait()

        first_unused = gs_ref[n_exp] // tile

        def ztail(b, carry):
            pltpu.make_async_copy(zero_ref, xs_hbm.at[pl.ds(pl.multiple_of(b * tile, tile), tile)], zsem).start()
            return carry

        def ztail_wait(b, carry):
            pltpu.make_async_copy(zero_ref, xs_hbm.at[pl.ds(0, tile)], zsem).wait()
            return carry

        lax.fori_loop(first_unused, n_blocks, ztail, 0)
        lax.fori_loop(first_unused, n_blocks, ztail_wait, 0)

    def row(t, carry):
        da = gs_ref[e0_ref[0, 0, t]] + r0_ref[0, 0, t]
        db = gs_ref[e1_ref[0, 0, t]] + r1_ref[0, 0, t]
        d0_ref[0, 0, t] = da
        d1_ref[0, 0, t] = db
        pltpu.make_async_copy(u_ref.at[pl.ds(t, 1)], xs_hbm.at[pl.ds(da, 1)], sem).start(priority=0)
        pltpu.make_async_copy(u_ref.at[pl.ds(t, 1)], xs_hbm.at[pl.ds(db, 1)], sem).start(priority=1)
        return carry

    lax.fori_loop(0, td, row, 0, unroll=4)
    for _ in range(2):
        pltpu.make_async_copy(u_ref, xs_hbm.at[pl.ds(0, td)], sem).wait()


def _dispatch(tot_i, e0, e1, r0, r1, u2, n_blocks, tile):
    t, d = u2.shape
    td = min(512, t)
    n_exp = N_EXPERTS
    nt = t // td
    resh = lambda a: a.reshape(nt, 1, td)
    smem_blk = pl.BlockSpec((1, 1, td), lambda i, tot: (i, 0, 0), memory_space=pltpu.SMEM)
    kern = functools.partial(_dispatch_kernel, n_exp=n_exp, n_blocks=n_blocks, tile=tile, td=td)
    grid_spec = pltpu.PrefetchScalarGridSpec(
        num_scalar_prefetch=1,
        grid=(nt,),
        in_specs=[smem_blk, smem_blk, smem_blk, smem_blk,
                  pl.BlockSpec((td, d), lambda i, tot: (i, 0))],
        out_specs=(pl.BlockSpec(memory_space=pl.ANY), smem_blk, smem_blk,
                   pl.BlockSpec((2, n_blocks), lambda i, tot: (0, 0), memory_space=pltpu.SMEM)),
        scratch_shapes=[pltpu.SMEM((n_exp + 1,), I32),
                        pltpu.VMEM((tile, d), F32),
                        pltpu.SemaphoreType.DMA(()),
                        pltpu.SemaphoreType.DMA(())],
    )
    xs, d0, d1, be = pl.pallas_call(
        kern,
        out_shape=(jax.ShapeDtypeStruct((n_blocks * tile, d), F32),
                   jax.ShapeDtypeStruct((nt, 1, td), I32),
                   jax.ShapeDtypeStruct((nt, 1, td), I32),
                   jax.ShapeDtypeStruct((2, n_blocks), I32)),
        grid_spec=grid_spec,
        compiler_params=_cparams(("arbitrary",)),
        name="moe_dispatch",
    )(tot_i, resh(e0), resh(e1), resh(r0), resh(r1), u2)
    return xs, d0.reshape(t), d1.reshape(t), be


def _expert_kernel(be_ref, x_ref, wg_ref, wu_ref, wd_ref, y_ref, xb_ref):
    b = pl.program_id(0)
    f = pl.program_id(1)
    valid = be_ref[1, b] == 1

    @pl.when(f == 0)
    def _():
        y_ref[...] = jnp.zeros_like(y_ref)

    @pl.when(valid)
    def _():
        @pl.when(f == 0)
        def _():
            xb_ref[...] = x_ref[...].astype(BF16)

        xb = xb_ref[...]
        hmid = _silu(_dot(xb, wg_ref[0])) * _dot(xb, wu_ref[0])
        y_ref[...] += _dot(hmid.astype(BF16), wd_ref[0])


def _expert_ffn(be, xs, w_gate, w_up, w_down, tile):
    rows, d = xs.shape
    n_blocks = rows // tile
    fdim = w_gate.shape[2]
    tf = 1024
    nf = fdim // tf

    def fsel(b, f, be_ref):
        return jnp.where(be_ref[1, b] == 1, f, nf - 1)

    grid_spec = pltpu.PrefetchScalarGridSpec(
        num_scalar_prefetch=1,
        grid=(n_blocks, nf),
        in_specs=[pl.BlockSpec((tile, d), lambda b, f, be_ref: (b * be_ref[1, b], 0)),
                  pl.BlockSpec((1, d, tf), lambda b, f, be_ref: (be_ref[0, b], 0, fsel(b, f, be_ref))),
                  pl.BlockSpec((1, d, tf), lambda b, f, be_ref: (be_ref[0, b], 0, fsel(b, f, be_ref))),
                  pl.BlockSpec((1, tf, d), lambda b, f, be_ref: (be_ref[0, b], fsel(b, f, be_ref), 0))],
        out_specs=pl.BlockSpec((tile, d), lambda b, f, be_ref: (b, 0)),
        scratch_shapes=[pltpu.VMEM((tile, d), BF16)],
    )
    return pl.pallas_call(
        _expert_kernel,
        out_shape=jax.ShapeDtypeStruct((rows, d), F32),
        grid_spec=grid_spec,
        compiler_params=_cparams(("parallel", "arbitrary")),
        name="moe_expert_ffn",
    )(be, xs, w_gate, w_up, w_down)


def _combine_kernel(d0_ref, d1_ref, n0_ref, n1_ref, rt_ref, x_ref, gate_ref, g_ref, b_ref, ys_hbm,
                    o_ref, buf_ref, sem, *, alpha, tc, nt):
    i = pl.program_id(0)
    slot = i % 2

    def gather(a_ref, b_ref2, s):
        def start(t, carry):
            pltpu.make_async_copy(ys_hbm.at[pl.ds(a_ref[0, 0, t], 1)],
                                  buf_ref.at[s, 0, pl.ds(t, 1)], sem.at[s]).start(priority=0)
            pltpu.make_async_copy(ys_hbm.at[pl.ds(b_ref2[0, 0, t], 1)],
                                  buf_ref.at[s, 1, pl.ds(t, 1)], sem.at[s]).start(priority=1)
            return carry

        lax.fori_loop(0, tc, start, 0, unroll=4)

    @pl.when(i == 0)
    def _():
        gather(d0_ref, d1_ref, 0)

    @pl.when(i + 1 < nt)
    def _():
        gather(n0_ref, n1_ref, 1 - slot)

    for half in range(2):
        pltpu.make_async_copy(ys_hbm.at[pl.ds(0, tc)], buf_ref.at[slot, half], sem.at[slot]).wait()
    rt = rt_ref[...]
    ffn = rt[:, 2:3] * buf_ref[slot, 0] + rt[:, 3:4] * buf_ref[slot, 1]
    z = alpha * x_ref[...] + gate_ref[0] * ffn
    o_ref[...] = _ln_rows(z) * g_ref[...] + b_ref[...]


def _combine(d0, d1, rt, x2, mod6, mod_row, seq, ln_g, ln_b, ys, alpha):
    t, d = x2.shape
    tc = min(256, seq)
    nt = t // tc
    per = seq // tc
    smem_blk = pl.BlockSpec((1, 1, tc), lambda i: (i, 0, 0), memory_space=pltpu.SMEM)
    next_blk = pl.BlockSpec((1, 1, tc), lambda i: (jnp.minimum(i + 1, nt - 1), 0, 0),
                            memory_space=pltpu.SMEM)
    d0r, d1r = d0.reshape(nt, 1, tc), d1.reshape(nt, 1, tc)
    return pl.pallas_call(
        functools.partial(_combine_kernel, alpha=alpha, tc=tc, nt=nt),
        out_shape=jax.ShapeDtypeStruct((t, d), F32),
        grid=(nt,),
        in_specs=[smem_blk, smem_blk, next_blk, next_blk,
                  pl.BlockSpec((tc, LANES), lambda i: (i, 0)),
                  pl.BlockSpec((tc, d), lambda i: (i, 0)),
                  pl.BlockSpec((1, 1, d), lambda i: (mod_row(i // per) * 6 + 5, 0, 0)),
                  pl.BlockSpec((1, d), lambda i: (0, 0)),
                  pl.BlockSpec((1, d), lambda i: (0, 0)),
                  pl.BlockSpec(memory_space=pl.ANY)],
        out_specs=pl.BlockSpec((tc, d), lambda i: (i, 0)),
        scratch_shapes=[pltpu.VMEM((2, 2, tc, d), F32), pltpu.SemaphoreType.DMA((2,))],
        compiler_params=_cparams(("arbitrary",)),
        name="moe_combine",
    )(d0r, d1r, d0r, d1r, rt, x2, mod6, ln_g.reshape(1, d), ln_b.reshape(1, d), ys)


def _moe_layer(xs, mod6, mod_row, w_router, w_gate, w_up, w_down, ln_g, ln_b, alpha):
    bsz, s, d = xs.shape
    t = bsz * s
    x2 = xs.reshape(t, d)
    tile = min(MOE_ROW_TILE, t)
    n_blocks = (2 * t) // tile + N_EXPERTS
    u2, rt, tot = _router(x2, mod6, mod_row, s, w_router)
    ri = rt[:, :8].astype(I32)
    tot_i = tot[0].astype(I32)
    xs_sorted, d0, d1, be = _dispatch(tot_i, ri[:, 0], ri[:, 1], ri[:, 4], ri[:, 5], u2, n_blocks, tile)
    ys = _expert_ffn(be, xs_sorted, w_gate, w_up, w_down, tile)
    out = _combine(d0, d1, rt, x2, mod6, mod_row, s, ln_g, ln_b, ys, alpha)
    return out.reshape(bsz, s, d)


def kernel(x, c, ctx, c_ctx, w_ada, b_ada, w_in, conv_w, conv_b, mlstm_gate_b, hgrn_lb,
           ret_decay_logit, head_norm_g, w_branch, w_out, post_ln_g, post_ln_b,
           ffn_w_gate, ffn_w_up, ffn_w_down, moe_w_router, moe_w_gate, moe_w_up, moe_w_down):
    bsz, s_lat, d = x.shape
    s_ctx = ctx.shape[1]
    depth = w_ada.shape[0]
    bw = d // 2
    n_gate = 4 * MLSTM_HEADS
    assert bsz < MOD_ROWS - 1 and s_lat % SCAN_CHUNK == 0 and s_ctx % SCAN_CHUNK == 0
    assert s_lat % GRID_W == 0 and (s_lat <= 1024 or s_lat % 1024 == 0)
    alpha = float((2 * depth) ** 0.25)
    ctx_row = MOD_ROWS // 2
    lat_rows = lambda b: b
    ctx_rows = lambda b: b * 0 + ctx_row

    col = {"mqk": 0, "mv": 2 * bw, "mz": 3 * bw, "hq": 4 * bw, "hi": 5 * bw, "hf": 6 * bw,
           "hg": 8 * bw, "rq": 9 * bw, "rk": 9 * bw + bw // 2, "rv": 10 * bw, "rg": 11 * bw,
           "merge": 12 * bw}
    g0 = 4 * bw

    cv = jnp.zeros((MOD_ROWS, d), F32).at[:bsz].set(c).at[ctx_row].set(c_ctx)
    mod = _ada(cv, w_ada, b_ada)

    h = ctx
    for l in range(depth):
        need_ctx = l < depth - 1
        mod6 = mod[l].reshape(MOD_ROWS * 6, 1, d)
        w_l = w_in[l]
        w_main = jnp.concatenate([w_l[:, :g0], w_l[:, g0 + n_gate:]], axis=1).astype(BF16)
        w_gate = jnp.pad(w_l[:, g0:g0 + n_gate], ((0, 0), (0, LANES - n_gate))).astype(BF16)

        y_x, g_x = _inproj(x, mod6, lat_rows, w_main, w_gate)
        y_h, g_h = _inproj(h, mod6, ctx_rows, w_main, w_gate)
        k_scale = float(bw // MLSTM_HEADS) ** -0.5
        qk = (_conv_silu(y_x, conv_w[l], conv_b[l], True, k_scale),
              _conv_silu(y_h, conv_w[l], conv_b[l], False, k_scale))
        y = (y_x, y_h)
        br_m = _mlstm(qk, y, (g_x, g_h), mlstm_gate_b[l], head_norm_g[l, 0], col["mv"], col["mz"])
        br_h = _hgrn(y, hgrn_lb, l, head_norm_g[l, 1], col["hq"], col["hi"], col["hf"], col["hg"], bw)
        br_r = _retention(y, ret_decay_logit[l], head_norm_g[l, 2],
                          col["rq"], col["rk"], col["rv"], col["rg"],
                          bw // RET_HEADS // 2, bw // RET_HEADS)
        w_br = w_branch[l].astype(BF16)
        w_o = w_out[l].astype(BF16)
        mixed = _merge_proj(br_m[0], br_h[0], br_r[0], y_x, w_br, col["merge"])
        x = _down(mixed, w_o, x, mod6, lat_rows, 2, post_ln_g[l, 0], post_ln_b[l, 0], alpha)
        if need_ctx:
            mixed_h = _merge_proj(br_m[1], br_h[1], br_r[1], y_h, w_br, col["merge"])
            h = _down(mixed_h, w_o, h, mod6, ctx_rows, 2, post_ln_g[l, 0], post_ln_b[l, 0], alpha)

        if l % 2 == 0:
            wg = ffn_w_gate[l // 2].astype(BF16)
            wu = ffn_w_up[l // 2].astype(BF16)
            wd = ffn_w_down[l // 2].astype(BF16)
            x = _down(_ffn_up(x, mod6, lat_rows, wg, wu), wd, x, mod6, lat_rows, 5,
                      post_ln_g[l, 1], post_ln_b[l, 1], alpha)
            if need_ctx:
                h = _down(_ffn_up(h, mod6, ctx_rows, wg, wu), wd, h, mod6, ctx_rows, 5,
                          post_ln_g[l, 1], post_ln_b[l, 1], alpha)
        else:
            e = l // 2
            wg = moe_w_gate[e].astype(BF16)
            wu = moe_w_up[e].astype(BF16)
            wd = moe_w_down[e].astype(BF16)
            x = _moe_layer(x, mod6, lat_rows, moe_w_router[e], wg, wu, wd,
                           post_ln_g[l, 1], post_ln_b[l, 1], alpha)
            if need_ctx:
                h = _moe_layer(h, mod6, ctx_rows, moe_w_router[e], wg, wu, wd,
                               post_ln_g[l, 1], post_ln_b[l, 1], alpha)
    return x
```

```python
import functools

import numpy as np
import jax
import jax.numpy as jnp
from jax import lax
from jax.experimental import pallas as pl
from jax.experimental.pallas import tpu as pltpu

F32 = jnp.float32
BF16 = jnp.bfloat16
I32 = jnp.int32

EPS = 1e-6
NEG = -1e30
TINY = 1e-30
LOG2_E = 1.4426950408889634

LANES = 128
SUBLANES = 8
SCAN_CHUNK = 256
VMEM_LIMIT_BYTES = 56 * 1024 * 1024
GRID_W = 64
MLSTM_HEADS = 4
HGRN_HEADS = 8
HGRN_HEADS_PER_STEP = 8
MLSTM_CHUNK = 256
HGRN_CHUNK = 128
RET_HEADS = 4
N_EXPERTS = 8
MOD_ROWS = 16


def _cparams(sem):
    return pltpu.CompilerParams(dimension_semantics=sem, vmem_limit_bytes=VMEM_LIMIT_BYTES)


def _dot(a, b):
    return jnp.dot(a, b, preferred_element_type=F32)


def _dot_nt(a, b):
    return lax.dot_general(a, b, (((1,), (1,)), ((), ())), preferred_element_type=F32)


def _dot_tn(a, b):
    return lax.dot_general(a, b, (((0,), (0,)), ((), ())), preferred_element_type=F32)


def _split_bf16(a):
    hi = a.astype(BF16)
    lo = (a - hi.astype(F32)).astype(BF16)
    return hi, lo


def _silu(a):
    return a * jax.nn.sigmoid(a)


def _ln_rows(a):
    mu = jnp.mean(a, axis=-1, keepdims=True)
    ac = a - mu
    var = jnp.mean(ac * ac, axis=-1, keepdims=True)
    return ac * lax.rsqrt(var + EPS)


def _ada_kernel(cv_ref, w_ref, b_ref, o_ref):
    a = _silu(cv_ref[...])
    a_hi, a_lo = _split_bf16(a)
    w_hi, w_lo = _split_bf16(w_ref[0])
    acc = _dot(a_hi, w_hi) + _dot(a_lo, w_hi) + _dot(a_hi, w_lo)
    o_ref[0] = acc + b_ref[0]


def _ada(cv, w_ada, b_ada):
    n_layers, d, n = w_ada.shape
    tn = 1024
    return pl.pallas_call(
        _ada_kernel,
        out_shape=jax.ShapeDtypeStruct((n_layers, MOD_ROWS, n), F32),
        grid=(n_layers, n // tn),
        in_specs=[pl.BlockSpec((MOD_ROWS, d), lambda l, j: (0, 0)),
                  pl.BlockSpec((1, d, tn), lambda l, j: (l, 0, j)),
                  pl.BlockSpec((1, 1, tn), lambda l, j: (l, 0, j))],
        out_specs=pl.BlockSpec((1, MOD_ROWS, tn), lambda l, j: (l, 0, j)),
        compiler_params=_cparams(("parallel", "parallel")),
        name="ada",
    )(cv, w_ada, b_ada.reshape(n_layers, 1, n))


def _inproj_kernel(x_ref, sh_ref, sc_ref, w_ref, wg_ref, y_ref, g_ref, u_ref):
    j = pl.program_id(2)

    @pl.when(j == 0)
    def _():
        u = _ln_rows(x_ref[0]) * (1.0 + sc_ref[0]) + sh_ref[0]
        ub = u.astype(BF16)
        u_ref[...] = ub
        g_ref[0] = _dot(ub, wg_ref[...])

    y_ref[0] = _dot(u_ref[...], w_ref[...]).astype(y_ref.dtype)


def _inproj(xs, mod6, mod_row, w_main, w_gate):
    bsz, s, d = xs.shape
    n = w_main.shape[1]
    tm = min(1024, s)
    tn = 1024
    row = mod_row
    return pl.pallas_call(
        _inproj_kernel,
        out_shape=(jax.ShapeDtypeStruct((bsz, s, n), BF16),
                   jax.ShapeDtypeStruct((bsz, s, LANES), F32)),
        grid=(bsz, s // tm, n // tn),
        in_specs=[pl.BlockSpec((1, tm, d), lambda b, i, j: (b, i, 0)),
                  pl.BlockSpec((1, 1, d), lambda b, i, j: (row(b) * 6 + 0, 0, 0)),
                  pl.BlockSpec((1, 1, d), lambda b, i, j: (row(b) * 6 + 1, 0, 0)),
                  pl.BlockSpec((d, tn), lambda b, i, j: (0, j)),
                  pl.BlockSpec((d, LANES), lambda b, i, j: (0, 0))],
        out_specs=(pl.BlockSpec((1, tm, tn), lambda b, i, j: (b, i, j)),
                   pl.BlockSpec((1, tm, LANES), lambda b, i, j: (b, i, 0))),
        scratch_shapes=[pltpu.VMEM((tm, d), BF16)],
        compiler_params=_cparams(("parallel", "parallel", "arbitrary")),
        name="inproj",
    )(xs, mod6, mod6, w_main, w_gate)


_CONV_PAD = 72


def _conv_kernel(a_ref, w_ref, b_ref, o_ref, pad_ref, *, seq, on_grid, k_scale, k_from):
    ct = a_ref.shape[2]
    zeros = jnp.zeros((_CONV_PAD, ct), F32)
    pad_ref[pl.ds(0, _CONV_PAD), :] = zeros
    pad_ref[pl.ds(_CONV_PAD + seq, _CONV_PAD), :] = zeros
    pad_ref[pl.ds(_CONV_PAD, seq), :] = a_ref[0].astype(F32)
    w = w_ref[...]
    bias = b_ref[...]
    scale = jnp.where(pl.program_id(1) >= k_from, k_scale, 1.0).astype(F32)

    rc = min(512, seq)
    for r0 in range(0, seq, rc):
        col = lax.broadcasted_iota(I32, (rc, 1), 0) % GRID_W
        acc = jnp.zeros((rc, ct), F32)
        for dr in range(3) if on_grid else (1,):
            for dc in range(3):
                off = (dr - 1) * GRID_W + (dc - 1)
                tap = pad_ref[pl.ds(_CONV_PAD + r0 + off, rc), :]
                if on_grid and dc == 0:
                    tap = jnp.where(col == 0, 0.0, tap)
                elif on_grid and dc == 2:
                    tap = jnp.where(col == GRID_W - 1, 0.0, tap)
                acc = acc + tap * w[dr * 3 + dc:dr * 3 + dc + 1, :]
        o_ref[0, pl.ds(r0, rc), :] = (_silu(acc + bias) * scale).astype(o_ref.dtype)


def _conv_silu(y, conv_w, conv_b, on_grid, k_scale):
    bsz, seq, _ = y.shape
    ch = conv_w.shape[-1]
    ct = 256
    kern = functools.partial(_conv_kernel, seq=seq, on_grid=on_grid, k_scale=k_scale,
                             k_from=(ch // 2) // ct)
    return pl.pallas_call(
        kern,
        out_shape=jax.ShapeDtypeStruct((bsz, seq, ch), BF16),
        grid=(bsz, ch // ct),
        in_specs=[pl.BlockSpec((1, seq, ct), lambda b, j: (b, 0, j)),
                  pl.BlockSpec((9, ct), lambda b, j: (0, j)),
                  pl.BlockSpec((1, ct), lambda b, j: (0, j))],
        out_specs=pl.BlockSpec((1, seq, ct), lambda b, j: (b, 0, j)),
        scratch_shapes=[pltpu.VMEM((seq + 2 * _CONV_PAD, ct), F32)],
        compiler_params=_cparams(("parallel", "parallel")),
        name="conv_silu",
    )(y, conv_w.reshape(9, ch), conv_b.reshape(1, ch))


def _sched(s, n_lat, n_ctx):
    nc = n_lat + n_ctx
    d = s // nc
    p = s - d * nc
    is_ctx = p < n_ctx
    pc = jnp.minimum(p, n_ctx - 1)
    pq = jnp.maximum(p - n_ctx, 0)
    ctx_c = jnp.where(d == 0, pc, n_ctx - 1 - pc)
    lat_c = jnp.where(d == 0, pq, n_lat - 1 - pq)
    return d, p, is_ctx, lat_c, ctx_c


def _scan_specs(n_lat, n_ctx, bp=1, c=SCAN_CHUNK):
    nc = n_lat + n_ctx

    def in_pair(width, col_blk):
        return (pl.BlockSpec((bp, c, width), lambda b, h, s: (b, _sched(s, n_lat, n_ctx)[3], col_blk(h, s))),
                pl.BlockSpec((bp, c, width), lambda b, h, s: (b, _sched(s, n_lat, n_ctx)[4], col_blk(h, s))))

    def out_pair(width, col_blk):
        return (pl.BlockSpec((bp, c, width),
                             lambda b, h, s: (b, _sched(jnp.maximum(s, nc), n_lat, n_ctx)[3], col_blk(h, s))),
                pl.BlockSpec((bp, c, width),
                             lambda b, h, s: (b, _sched(jnp.maximum(s, nc), n_lat, n_ctx)[4], col_blk(h, s))))

    return in_pair, out_pair


def _store_scan_out(is_ctx, val, ol_ref, oc_ref, bi=0):
    @pl.when(is_ctx)
    def _():
        oc_ref[bi] = val.astype(oc_ref.dtype)

    @pl.when(jnp.logical_not(is_ctx))
    def _():
        ol_ref[bi] = val.astype(ol_ref.dtype)


SCAN_BATCH_PER_STEP = 2


def _log_sigmoid(a):
    return jnp.minimum(a, 0.0) - jnp.log1p(jnp.exp(-jnp.abs(a)))


def _prefix_sums(cols, rows, tri_b, nh):
    lane = lax.broadcasted_iota(I32, (1, LANES), 1)
    hi = cols.astype(BF16).astype(F32)
    rhs = jnp.where(lane < nh, hi, pltpu.roll(cols - hi, nh, axis=1)).astype(BF16)
    bc = _dot(tri_b, rhs)
    col = bc + pltpu.roll(bc, LANES - nh, axis=1)
    sub = lax.broadcasted_iota(I32, (SUBLANES, 1), 0)
    rhi = rows.astype(BF16).astype(F32)
    lhs = jnp.where(sub < nh, rhi, pltpu.roll(rows - rhi, nh, axis=0))
    lhs = jnp.concatenate([lhs, jnp.zeros_like(lhs)], axis=0).astype(BF16)
    br = _dot_nt(lhs, tri_b)[0:SUBLANES]
    row = br + pltpu.roll(br, SUBLANES - nh, axis=0)
    return col, row


def _mlstm_kernel(ql_ref, qc_ref, kl_ref, kc_ref, vl_ref, vc_ref, zl_ref, zc_ref, gl_ref, gc_ref,
                  gb_ref, ng_ref, ol_ref, oc_ref, c_ref, n_ref, m_ref, hbuf_ref, *, n_lat, n_ctx):
    d, p, is_ctx, lat_c, ctx_c = _sched(pl.program_id(2), n_lat, n_ctx)

    @pl.when(p == 0)
    def _():
        c_ref[...] = jnp.zeros_like(c_ref)
        n_ref[...] = jnp.zeros_like(n_ref)
        m_ref[...] = jnp.full_like(m_ref, NEG)

    for bi in range(c_ref.shape[0]):
        _mlstm_step(bi, d, is_ctx, lat_c, ctx_c, ql_ref, qc_ref, kl_ref, kc_ref, vl_ref, vc_ref,
                    zl_ref, zc_ref, gl_ref, gc_ref, gb_ref, ng_ref, ol_ref, oc_ref,
                    c_ref.at[bi], n_ref.at[bi], m_ref.at[bi], hbuf_ref.at[bi], n_lat)


def _mlstm_step(bi, d, is_ctx, lat_c, ctx_c, ql_ref, qc_ref, kl_ref, kc_ref, vl_ref, vc_ref,
                zl_ref, zc_ref, gl_ref, gc_ref, gb_ref, ng_ref, ol_ref, oc_ref,
                c_ref, n_ref, m_ref, hbuf_ref, n_lat):
    nh = MLSTM_HEADS
    chunk = jnp.where(is_ctx, n_lat + ctx_c, lat_c)
    pick = lambda a_l, a_c: jnp.where(is_ctx, a_c[bi], a_l[bi])
    c = ql_ref.shape[1]
    dh = c_ref.shape[1]

    gates = pick(gl_ref, gc_ref) + gb_ref[...]
    gates = jnp.where(d == 0, gates, pltpu.roll(gates, LANES - 2 * nh, axis=1))
    lane = lax.broadcasted_iota(I32, (1, LANES), 1)
    r32 = jnp.where(lane < nh, gates, jnp.where(lane < 2 * nh, _log_sigmoid(gates), 0.0))
    rt = r32.T[0:SUBLANES]
    lf_cols = jnp.where(lane < nh, pltpu.roll(r32, LANES - nh, axis=1), 0.0)
    sub = lax.broadcasted_iota(I32, (SUBLANES, 1), 0)
    lf_rows = jnp.where(sub < nh, pltpu.roll(rt, SUBLANES - nh, axis=0), 0.0)

    t_i = lax.broadcasted_iota(I32, (c, c), 0)
    s_i = lax.broadcasted_iota(I32, (c, c), 1)
    tri = (t_i - s_i) * (1 - 2 * d) >= 0
    b_cols, b_rows = _prefix_sums(lf_cols, lf_rows, tri.astype(BF16), nh)
    totals = jnp.sum(lf_cols, axis=0, keepdims=True)

    q_all = pick(ql_ref, qc_ref)
    k_all = pick(kl_ref, kc_ref)
    v_all = pick(vl_ref, vc_ref)
    outs = []
    for j in range(nh):
        li = r32[:, j:j + 1]
        li_row = rt[j:j + 1, :]
        b_col = b_cols[:, j:j + 1]
        b_row = b_rows[j:j + 1, :]
        total = totals[:, j:j + 1]
        q = q_all[:, j * dh:(j + 1) * dh]
        k = k_all[:, j * dh:(j + 1) * dh]
        vb = v_all[:, j * dh:(j + 1) * dh].astype(BF16)

        m_prev = m_ref[j, 0:1, 0:1]
        d_mat = jnp.where(tri, b_col - b_row + li_row, NEG)
        a_inter = b_col + m_prev
        m_t = jnp.maximum(a_inter, jnp.max(d_mat, axis=1, keepdims=True))
        w_inter = jnp.exp(a_inter - m_t)
        s_mat = _dot_nt(q, k) * jnp.exp(d_mat - m_t)
        num = w_inter * _dot(q, c_ref[j].astype(BF16)) + _dot(s_mat.astype(BF16), vb)
        den = (w_inter * jnp.sum(q.astype(F32) * n_ref[j, 0:1, :], axis=1, keepdims=True)
               + jnp.sum(s_mat, axis=1, keepdims=True))
        outs.append(num / jnp.maximum(jnp.abs(den), jnp.exp(-m_t)))

        g_col = total - b_col + li
        m_new = jnp.maximum(total + m_prev, jnp.max(g_col, axis=0, keepdims=True))
        decay = jnp.exp(total + m_prev - m_new)
        kw = k.astype(F32) * jnp.exp(g_col - m_new)
        c_ref[j] = decay * c_ref[j] + _dot_tn(kw.astype(BF16), vb)
        n_new = decay * n_ref[j, 0:1, :] + jnp.sum(kw, axis=0, keepdims=True)
        n_ref[j] = jnp.broadcast_to(n_new, n_ref.shape[1:])
        m_ref[j] = jnp.broadcast_to(m_new, m_ref.shape[1:])

    @pl.when(d == 0)
    def _():
        for j in range(nh):
            hbuf_ref[chunk, :, j * dh:(j + 1) * dh] = outs[j].astype(hbuf_ref.dtype)

    @pl.when(d == 1)
    def _():
        z = pick(zl_ref, zc_ref).astype(F32)
        fin = []
        for j in range(nh):
            sl = slice(j * dh, (j + 1) * dh)
            hn = _ln_rows(hbuf_ref[chunk, :, sl].astype(F32) + outs[j])
            fin.append(hn * ng_ref[:, sl] * _silu(z[:, sl]))
        _store_scan_out(is_ctx, jnp.concatenate(fin, axis=1), ol_ref, oc_ref, bi)


def _mlstm(qk, y, g, gate_b, norm_g, v_col, z_col):
    bsz, s_lat, w2 = qk[0].shape
    s_ctx = qk[1].shape[1]
    width = w2 // 2
    dh = width // MLSTM_HEADS
    c = MLSTM_CHUNK
    n_lat, n_ctx = s_lat // c, s_ctx // c
    bp = 1
    in_pair, out_pair = _scan_specs(n_lat, n_ctx, bp, c)
    kern = functools.partial(_mlstm_kernel, n_lat=n_lat, n_ctx=n_ctx)
    gb = jnp.zeros((1, LANES), F32).at[0, :gate_b.shape[0]].set(gate_b)
    return pl.pallas_call(
        kern,
        out_shape=(jax.ShapeDtypeStruct((bsz, s_lat, width), BF16),
                   jax.ShapeDtypeStruct((bsz, s_ctx, width), BF16)),
        grid=(bsz // bp, 1, 2 * (n_lat + n_ctx)),
        in_specs=[*in_pair(width, lambda h, s: 0),
                  *in_pair(width, lambda h, s: 1),
                  *in_pair(width, lambda h, s: v_col // width),
                  *out_pair(width, lambda h, s: z_col // width),
                  *in_pair(LANES, lambda h, s: 0),
                  pl.BlockSpec((1, LANES), lambda b, h, s: (0, 0)),
                  pl.BlockSpec((1, width), lambda b, h, s: (0, 0))],
        out_specs=out_pair(width, lambda h, s: 0),
        scratch_shapes=[pltpu.VMEM((bp, MLSTM_HEADS, dh, dh), F32),
                        pltpu.VMEM((bp, MLSTM_HEADS, SUBLANES, dh), F32),
                        pltpu.VMEM((bp, MLSTM_HEADS, SUBLANES, LANES), F32),
                        pltpu.VMEM((bp, n_lat + n_ctx, c, width), BF16)],
        compiler_params=_cparams(("parallel", "parallel", "arbitrary")),
        name="mlstm",
    )(qk[0], qk[1], qk[0], qk[1], y[0], y[1], y[0], y[1], g[0], g[1], gb, norm_g.reshape(1, width))


def _ret_kernel(ql_ref, qc_ref, kl_ref, kc_ref, vl_ref, vc_ref, zl_ref, zc_ref, dl_ref, ng_ref,
                ol_ref, oc_ref, s_ref, dmat_ref, hbuf_ref, *, n_lat, n_ctx, q_scale):
    nh = RET_HEADS
    d, p, is_ctx, lat_c, ctx_c = _sched(pl.program_id(2), n_lat, n_ctx)
    chunk = jnp.where(is_ctx, n_lat + ctx_c, lat_c)
    c = SCAN_CHUNK
    bp, dk, dv = s_ref.shape[0], s_ref.shape[2], s_ref.shape[3]

    lane = lax.broadcasted_iota(I32, (1, LANES), 1)
    lgs = _log_sigmoid(dl_ref[...])
    lg = [jnp.sum(jnp.where(lane == d * nh + j, lgs, 0.0), axis=1, keepdims=True) for j in range(nh)]

    @pl.when(p == 0)
    def _():
        s_ref[...] = jnp.zeros_like(s_ref)
        t_i = lax.broadcasted_iota(I32, (c, c), 0)
        s_i = lax.broadcasted_iota(I32, (c, c), 1)
        dist = (t_i - s_i) * (1 - 2 * d)
        distf = jnp.maximum(dist, 0).astype(F32)
        for j in range(nh):
            dmat_ref[j] = jnp.where(dist >= 0, jnp.exp(lg[j] * distf), 0.0)

    t_c = lax.broadcasted_iota(I32, (c, 1), 0)
    pos = jnp.where(d == 0, t_c + 1, c - t_c).astype(F32)
    inter_w = [jnp.exp(lg[j] * pos) for j in range(nh)]
    state_w = [jnp.exp(lg[j] * (c - pos)) for j in range(nh)]
    for bi in range(bp):
        pick = lambda a_l, a_c: jnp.where(is_ctx, a_c[bi], a_l[bi])
        q_all = pick(ql_ref, qc_ref)
        k_all = pick(kl_ref, kc_ref)
        v_all = pick(vl_ref, vc_ref)
        outs = []
        for j in range(nh):
            qb = (q_all[:, j * dk:(j + 1) * dk].astype(F32) * q_scale).astype(BF16)
            kf = k_all[:, j * dk:(j + 1) * dk].astype(F32)
            vb = v_all[:, j * dv:(j + 1) * dv].astype(BF16)
            a = _dot_nt(qb, kf.astype(BF16)) * dmat_ref[j]
            outs.append(inter_w[j] * _dot(qb, s_ref[bi, j].astype(BF16)) + _dot(a.astype(BF16), vb))
            s_ref[bi, j] = (jnp.exp(lg[j] * c) * s_ref[bi, j]
                            + _dot_tn((kf * state_w[j]).astype(BF16), vb))

        @pl.when(d == 0)
        def _():
            for j in range(nh):
                hbuf_ref[bi, chunk, :, j * dv:(j + 1) * dv] = outs[j].astype(hbuf_ref.dtype)

        @pl.when(d == 1)
        def _():
            z = pick(zl_ref, zc_ref).astype(F32)
            fin = []
            for j in range(nh):
                sl = slice(j * dv, (j + 1) * dv)
                hn = _ln_rows(hbuf_ref[bi, chunk, :, sl].astype(F32) + outs[j])
                fin.append(hn * ng_ref[:, sl] * _silu(z[:, sl]))
            _store_scan_out(is_ctx, jnp.concatenate(fin, axis=1), ol_ref, oc_ref, bi)


def _retention(y, decay_logit, norm_g, q_col, k_col, v_col, z_col, dk, dv):
    bsz, s_lat, _ = y[0].shape
    s_ctx = y[1].shape[1]
    c = SCAN_CHUNK
    n_lat, n_ctx = s_lat // c, s_ctx // c
    bp = SCAN_BATCH_PER_STEP if bsz % SCAN_BATCH_PER_STEP == 0 else 1
    in_pair, out_pair = _scan_specs(n_lat, n_ctx, bp)
    kern = functools.partial(_ret_kernel, n_lat=n_lat, n_ctx=n_ctx, q_scale=float(dk) ** -0.5)
    wk, wv = RET_HEADS * dk, RET_HEADS * dv
    dl = jnp.zeros((1, LANES), F32).at[0, :2 * RET_HEADS].set(decay_logit.reshape(-1))
    return pl.pallas_call(
        kern,
        out_shape=(jax.ShapeDtypeStruct((bsz, s_lat, RET_HEADS * dv), BF16),
                   jax.ShapeDtypeStruct((bsz, s_ctx, RET_HEADS * dv), BF16)),
        grid=(bsz // bp, 1, 2 * (n_lat + n_ctx)),
        in_specs=[*in_pair(wk, lambda h, s: q_col // wk),
                  *in_pair(wk, lambda h, s: k_col // wk),
                  *in_pair(wv, lambda h, s: v_col // wv),
                  *out_pair(wv, lambda h, s: z_col // wv),
                  pl.BlockSpec((1, LANES), lambda b, h, s: (0, 0)),
                  pl.BlockSpec((1, wv), lambda b, h, s: (0, 0))],
        out_specs=out_pair(wv, lambda h, s: 0),
        scratch_shapes=[pltpu.VMEM((bp, RET_HEADS, dk, dv), F32),
                        pltpu.VMEM((RET_HEADS, c, c), F32),
                        pltpu.VMEM((bp, n_lat + n_ctx, c, wv), BF16)],
        compiler_params=_cparams(("parallel", "parallel", "arbitrary")),
        name="retention",
    )(y[0], y[1], y[0], y[1], y[0], y[1], y[0], y[1], dl, norm_g.reshape(1, RET_HEADS * dv))


def _hgrn_tables(c):
    nl = int(np.log2(c))
    t = np.arange(c)[:, None]
    u = np.arange(c)[None, :]
    x = (t ^ u).astype(np.int64)
    lvl = np.where(x == 0, nl, np.floor(np.log2(np.maximum(x, 1))).astype(np.int64))
    lvl2 = np.stack([np.where(u <= t, lvl, -1), np.where(u >= t, lvl, -1)])
    return (jnp.asarray((u <= t).astype(np.float32), BF16), jnp.asarray(lvl2, I32), nl)


def _block_ref_rows(b, lev):
    c, w = b.shape
    n = 1 << lev
    if 2 * n >= SUBLANES:
        b3 = b.reshape(c // (2 * n), 2 * n, w)
        return jnp.broadcast_to(b3[:, n - 1:n, :], b3.shape).reshape(c, w)
    t = lax.broadcasted_iota(I32, (c, 1), 0) & (2 * n - 1)
    out = b
    for k in range(2 * n):
        if k != n - 1:
            out = jnp.where(t == k, pltpu.roll(b, (k - (n - 1)) % c, axis=0), out)
    return out


def _hgrn_kernel(ql_ref, qc_ref, vl_ref, vc_ref, fl_ref, fc_ref, zl_ref, zc_ref, lb_ref, ng_ref,
                 tab_ref, lvl_ref, ol_ref, oc_ref, st_ref, hbuf_ref, *, n_lat, n_ctx, layer, nl):
    d, p, is_ctx, lat_c, ctx_c = _sched(pl.program_id(2), n_lat, n_ctx)
    chunk = jnp.where(is_ctx, n_lat + ctx_c, lat_c)
    pick = lambda a_l, a_c: jnp.where(is_ctx, a_c[0], a_l[0])
    c = tab_ref.shape[0]

    @pl.when(p == 0)
    def _():
        st_ref[...] = jnp.zeros_like(st_ref)

    lbp = lb_ref[0]
    sm = jnp.exp(lbp - jnp.max(lbp, axis=0, keepdims=True))
    sm = sm / jnp.sum(sm, axis=0, keepdims=True)
    lb = jnp.zeros((1, lbp.shape[1]), F32)
    for i in range(1, layer + 1):
        lb = lb + sm[i:i + 1, :]

    ft = pick(fl_ref, fc_ref).astype(F32)
    sg = jax.nn.sigmoid(ft)
    kk = (1.0 - lb) * (1.0 - sg)
    lf = jnp.log(jnp.maximum(lb + (1.0 - lb) * sg, TINY)) * LOG2_E
    q = _silu(pick(ql_ref, qc_ref).astype(F32))
    v_all = pick(vl_ref, vc_ref)
    w = lf.shape[1]
    dk = st_ref.shape[1]
    nh = w // dk
    lf_hi = lf.astype(BF16)
    r1 = lf - lf_hi.astype(F32)
    lf_mid = r1.astype(BF16)
    lf_lo = (r1 - lf_mid.astype(F32)).astype(BF16)
    ps = _dot(tab_ref[...], jnp.concatenate([lf_hi, lf_mid, lf_lo], axis=1))
    b = ps[:, :w] + ps[:, w:2 * w] + ps[:, 2 * w:]
    total = jnp.sum(lf, axis=0, keepdims=True)
    bx = b - jnp.where(d == 0, 0.0, 1.0) * lf

    def seg_exp(idx):
        if idx < nl:
            return jnp.exp2(-jnp.abs(bx - _block_ref_rows(b, idx)))
        query_side = (idx == nl)
        from_start = jnp.where(d == 0, 1.0, 0.0) if query_side else jnp.where(d == 0, 0.0, 1.0)
        return jnp.exp2(from_start * bx + (1.0 - from_start) * (total - bx))

    t_c = lax.broadcasted_iota(I32, (c, 1), 0)
    lvl = lvl_ref[0]
    hs = [slice(j * dk, (j + 1) * dk) for j in range(nh)]
    qb = q.astype(BF16)
    kb = kk.astype(BF16)
    accs = [jnp.where(lvl == nl, _dot_nt(qb[:, sl], kb[:, sl]), 0.0) for sl in hs]
    for lev in range(nl):
        is_q = ((t_c >> lev) & 1) != d
        z = (jnp.where(is_q, q, kk) * seg_exp(lev)).astype(BF16)
        hit = lvl == lev
        accs = [jnp.where(hit, _dot_nt(z[:, sl], z[:, sl]), a) for sl, a in zip(hs, accs)]

    qg = (q * seg_exp(nl)).astype(BF16)
    kg = (kk * seg_exp(nl + 1)).astype(BF16)
    decay = jnp.exp2(total)
    outs = []
    for j, sl in enumerate(hs):
        st = st_ref[j]
        vb = v_all[:, sl].astype(BF16)
        outs.append(_dot(accs[j].astype(BF16), vb) + _dot_nt(qg[:, sl], st.astype(BF16)))
        st_ref[j] = st * decay[:, sl] + _dot_tn(vb, kg[:, sl])

    @pl.when(d == 0)
    def _():
        for j, sl in enumerate(hs):
            hbuf_ref[chunk, :, sl] = outs[j]

    @pl.when(d == 1)
    def _():
        z = pick(zl_ref, zc_ref).astype(F32)
        fin = []
        for j, sl in enumerate(hs):
            tot = hbuf_ref[chunk, :, sl] + outs[j]
            hn = tot * lax.rsqrt(jnp.mean(tot * tot, axis=-1, keepdims=True) + EPS)
            fin.append(hn * ng_ref[:, sl] * _silu(z[:, sl]))
        _store_scan_out(is_ctx, jnp.concatenate(fin, axis=1), ol_ref, oc_ref)


def _hgrn(y, hgrn_lb, layer, norm_g, q_col, v_col, f_col, z_col, width):
    bsz, s_lat, _ = y[0].shape
    s_ctx = y[1].shape[1]
    dk = width // HGRN_HEADS
    depth = hgrn_lb.shape[1]
    c = HGRN_CHUNK
    n_lat, n_ctx = s_lat // c, s_ctx // c
    nc = n_lat + n_ctx
    tabs, lvl, nl = _hgrn_tables(c)
    in_pair, out_pair = _scan_specs(n_lat, n_ctx, 1, c)
    kern = functools.partial(_hgrn_kernel, n_lat=n_lat, n_ctx=n_ctx, layer=layer, nl=nl)
    hp = HGRN_HEADS_PER_STEP
    wg = hp * dk
    per_dir = width // wg
    return pl.pallas_call(
        kern,
        out_shape=(jax.ShapeDtypeStruct((bsz, s_lat, width), BF16),
                   jax.ShapeDtypeStruct((bsz, s_ctx, width), BF16)),
        grid=(bsz, HGRN_HEADS // hp, 2 * nc),
        in_specs=[*in_pair(wg, lambda h, s: q_col // wg + h),
                  *in_pair(wg, lambda h, s: v_col // wg + h),
                  *in_pair(wg, lambda h, s: f_col // wg + (s // nc) * per_dir + h),
                  *out_pair(wg, lambda h, s: z_col // wg + h),
                  pl.BlockSpec((1, depth, wg), lambda b, h, s: (s // nc, 0, h)),
                  pl.BlockSpec((1, wg), lambda b, h, s: (0, h)),
                  pl.BlockSpec((c, c), lambda b, h, s: (0, 0)),
                  pl.BlockSpec((1, c, c), lambda b, h, s: (s // nc, 0, 0))],
        out_specs=out_pair(wg, lambda h, s: h),
        scratch_shapes=[pltpu.VMEM((hp, dk, dk), F32),
                        pltpu.VMEM((nc, c, wg), F32)],
        compiler_params=_cparams(("parallel", "parallel", "arbitrary")),
        name="hgrn2",
    )(y[0], y[1], y[0], y[1], y[0], y[1], y[0], y[1], hgrn_lb, norm_g.reshape(1, width), tabs, lvl)


def _merge_kernel(bm_ref, bh_ref, br_ref, g0_ref, g1_ref, g2_ref, w_ref, o_ref):
    acc = jax.nn.sigmoid(g0_ref[0].astype(F32)) * _dot(bm_ref[0], w_ref[0])
    acc = acc + jax.nn.sigmoid(g1_ref[0].astype(F32)) * _dot(bh_ref[0], w_ref[1])
    acc = acc + jax.nn.sigmoid(g2_ref[0].astype(F32)) * _dot(br_ref[0], w_ref[2])
    o_ref[0] = acc.astype(o_ref.dtype)


def _merge_proj(br_m, br_h, br_r, y, w_br, merge_col):
    bsz, s, width = br_m.shape
    d = w_br.shape[2]
    tm = min(1024, s)
    tn = 512
    gcol = merge_col // tn
    nj = d // tn
    bspec = pl.BlockSpec((1, tm, width), lambda b, i, j: (b, i, 0))

    def gspec(n):
        return pl.BlockSpec((1, tm, tn), lambda b, i, j: (b, i, gcol + n * nj + j))

    return pl.pallas_call(
        _merge_kernel,
        out_shape=jax.ShapeDtypeStruct((bsz, s, d), BF16),
        grid=(bsz, s // tm, nj),
        in_specs=[bspec, bspec, bspec, gspec(0), gspec(1), gspec(2),
                  pl.BlockSpec((3, width, tn), lambda b, i, j: (0, 0, j))],
        out_specs=pl.BlockSpec((1, tm, tn), lambda b, i, j: (b, i, j)),
        compiler_params=_cparams(("parallel", "parallel", "arbitrary")),
        name="merge_proj",
    )(br_m, br_h, br_r, y, y, y, w_br)


def _down_kernel(a_ref, w_ref, x_ref, gate_ref, g_ref, b_ref, o_ref, *, alpha, nk):
    k = pl.program_id(2)

    @pl.when(k == 0)
    def _():
        o_ref[...] = jnp.zeros_like(o_ref)

    o_ref[0] += _dot(a_ref[0], w_ref[...])

    @pl.when(k == nk - 1)
    def _():
        z = alpha * x_ref[0] + gate_ref[0] * o_ref[0]
        o_ref[0] = _ln_rows(z) * g_ref[...] + b_ref[...]


def _down(a, w, xs, mod6, mod_row, which, ln_g, ln_b, alpha):
    bsz, s, d = xs.shape
    kdim = w.shape[0]
    tk = next(t for t in (2048, 1024, 512, kdim) if kdim % t == 0)
    nk = kdim // tk
    tm = min(1024, s)
    row = mod_row
    kern = functools.partial(_down_kernel, alpha=alpha, nk=nk)
    return pl.pallas_call(
        kern,
        out_shape=jax.ShapeDtypeStruct((bsz, s, d), F32),
        grid=(bsz, s // tm, nk),
        in_specs=[pl.BlockSpec((1, tm, tk), lambda b, i, k: (b, i, k)),
                  pl.BlockSpec((tk, d), lambda b, i, k: (k, 0)),
                  pl.BlockSpec((1, tm, d), lambda b, i, k: (b, i, 0)),
                  pl.BlockSpec((1, 1, d), lambda b, i, k: (row(b) * 6 + which, 0, 0)),
                  pl.BlockSpec((1, d), lambda b, i, k: (0, 0)),
                  pl.BlockSpec((1, d), lambda b, i, k: (0, 0))],
        out_specs=pl.BlockSpec((1, tm, d), lambda b, i, k: (b, i, 0)),
        compiler_params=_cparams(("parallel", "parallel", "arbitrary")),
        name="down_postnorm",
    )(a, w, xs, mod6, ln_g.reshape(1, d), ln_b.reshape(1, d))


def _ffn_up_kernel(x_ref, sh_ref, sc_ref, wg_ref, wu_ref, h_ref, u_ref):
    @pl.when(pl.program_id(2) == 0)
    def _():
        u = _ln_rows(x_ref[0]) * (1.0 + sc_ref[0]) + sh_ref[0]
        u_ref[...] = u.astype(BF16)

    ub = u_ref[...]
    h_ref[0] = (_silu(_dot(ub, wg_ref[...])) * _dot(ub, wu_ref[...])).astype(h_ref.dtype)


def _ffn_up(xs, mod6, mod_row, w_gate, w_up):
    bsz, s, d = xs.shape
    f = w_gate.shape[1]
    tm = min(1024, s)
    tn = 512
    row = mod_row
    return pl.pallas_call(
        _ffn_up_kernel,
        out_shape=jax.ShapeDtypeStruct((bsz, s, f), BF16),
        grid=(bsz, s // tm, f // tn),
        in_specs=[pl.BlockSpec((1, tm, d), lambda b, i, j: (b, i, 0)),
                  pl.BlockSpec((1, 1, d), lambda b, i, j: (row(b) * 6 + 3, 0, 0)),
                  pl.BlockSpec((1, 1, d), lambda b, i, j: (row(b) * 6 + 4, 0, 0)),
                  pl.BlockSpec((d, tn), lambda b, i, j: (0, j)),
                  pl.BlockSpec((d, tn), lambda b, i, j: (0, j))],
        out_specs=pl.BlockSpec((1, tm, tn), lambda b, i, j: (b, i, j)),
        scratch_shapes=[pltpu.VMEM((tm, d), BF16)],
        compiler_params=_cparams(("parallel", "parallel", "arbitrary")),
        name="ffn_up",
    )(xs, mod6, mod6, w_gate, w_up)


MOE_ROW_TILE = 512


def _router_kernel(x_ref, sh_ref, sc_ref, w_ref, u_ref, rt_ref, tot_ref, carry_ref, *, n_exp):
    i = pl.program_id(0)

    @pl.when(i == 0)
    def _():
        carry_ref[...] = jnp.zeros_like(carry_ref)

    u = _ln_rows(x_ref[...]) * (1.0 + sc_ref[0]) + sh_ref[0]
    u_ref[...] = u
    u_hi, u_lo = _split_bf16(u)
    w_hi, w_lo = _split_bf16(w_ref[...])
    logits = _dot(u_hi, w_hi) + _dot(u_lo, w_hi) + _dot(u_hi, w_lo)
    tr = logits.shape[0]
    lane = lax.broadcasted_iota(I32, (1, LANES), 1)
    ninf = jnp.float32(-jnp.inf)
    lg = jnp.where(lane < n_exp, logits, ninf)
    m1 = jnp.max(lg, axis=1, keepdims=True)
    i1 = jnp.min(jnp.where(lg == m1, lane, LANES), axis=1, keepdims=True)
    lg2 = jnp.where(lane == i1, ninf, lg)
    m2 = jnp.max(lg2, axis=1, keepdims=True)
    i2 = jnp.min(jnp.where(lg2 == m2, lane, LANES), axis=1, keepdims=True)
    e = jnp.exp(m2 - m1)
    w0 = 1.0 / (1.0 + e)
    w1 = e / (1.0 + e)

    oh = ((lane == i1) | (lane == i2)).astype(F32)
    r_i = lax.broadcasted_iota(I32, (tr, tr), 0)
    c_i = lax.broadcasted_iota(I32, (tr, tr), 1)
    cum = _dot((c_i < r_i).astype(BF16), oh.astype(BF16)) + carry_ref[0:1, :]
    r0 = jnp.sum(jnp.where(lane == i1, cum, 0.0), axis=1, keepdims=True)
    r1 = jnp.sum(jnp.where(lane == i2, cum, 0.0), axis=1, keepdims=True)
    new_carry = carry_ref[0:1, :] + jnp.sum(oh, axis=0, keepdims=True)
    carry_ref[...] = jnp.broadcast_to(new_carry, carry_ref.shape)
    tot_ref[...] = jnp.broadcast_to(new_carry, tot_ref.shape)

    rt = jnp.where(lane == 0, i1.astype(F32), 0.0)
    rt = jnp.where(lane == 1, i2.astype(F32), rt)
    rt = jnp.where(lane == 2, w0, rt)
    rt = jnp.where(lane == 3, w1, rt)
    rt = jnp.where(lane == 4, r0, rt)
    rt = jnp.where(lane == 5, r1, rt)
    rt_ref[...] = rt


def _router(x2, mod6, mod_row, seq, w_router):
    t, d = x2.shape
    n_exp = w_router.shape[1]
    tr = min(512, seq)
    wr = jnp.zeros((d, LANES), F32).at[:, :n_exp].set(w_router)
    kern = functools.partial(_router_kernel, n_exp=n_exp)
    per = seq // tr
    return pl.pallas_call(
        kern,
        out_shape=(jax.ShapeDtypeStruct((t, d), F32),
                   jax.ShapeDtypeStruct((t, LANES), F32),
                   jax.ShapeDtypeStruct((SUBLANES, LANES), F32)),
        grid=(t // tr,),
        in_specs=[pl.BlockSpec((tr, d), lambda i: (i, 0)),
                  pl.BlockSpec((1, 1, d), lambda i: (mod_row(i // per) * 6 + 3, 0, 0)),
                  pl.BlockSpec((1, 1, d), lambda i: (mod_row(i // per) * 6 + 4, 0, 0)),
                  pl.BlockSpec((d, LANES), lambda i: (0, 0))],
        out_specs=(pl.BlockSpec((tr, d), lambda i: (i, 0)),
                   pl.BlockSpec((tr, LANES), lambda i: (i, 0)),
                   pl.BlockSpec((SUBLANES, LANES), lambda i: (0, 0))),
        scratch_shapes=[pltpu.VMEM((SUBLANES, LANES), F32)],
        compiler_params=_cparams(("arbitrary",)),
        name="moe_router",
    )(x2, mod6, mod6, wr)


def _dispatch_kernel(tot_ref, e0_ref, e1_ref, r0_ref, r1_ref, u_ref,
                     xs_hbm, d0_ref, d1_ref, be_ref, gs_ref, zero_ref, sem, zsem,
                     *, n_exp, n_blocks, tile, td):
    i = pl.program_id(0)

    @pl.when(i == 0)
    def _():
        start = jnp.int32(0)
        for e in range(n_exp):
            gs_ref[e] = start
            start = start + ((tot_ref[e] + (tile - 1)) // tile) * tile
        gs_ref[n_exp] = start

        def blk(b, carry):
            row = b * tile
            ex = jnp.int32(0)
            for e in range(1, n_exp):
                ex = jnp.where(row >= gs_ref[e], e, ex)
            be_ref[0, b] = ex
            be_ref[1, b] = jnp.where(row < gs_ref[n_exp], 1, 0).astype(I32)
            return carry

        lax.fori_loop(0, n_blocks, blk, 0)

        zero_ref[...] = jnp.zeros_like(zero_ref)
        for e in range(n_exp):
            last = pl.multiple_of(jnp.maximum(gs_ref[e + 1] - tile, 0), tile)
            pltpu.make_async_copy(zero_ref, xs_hbm.at[pl.ds(last, tile)], zsem).start()
        for e in range(n_exp):
            pltpu.make_async_copy(zero_ref, xs_hbm.at[pl.ds(0, tile)], zsem).wait()

        first_unused = gs_ref[n_exp] // tile

        def ztail(b, carry):
            pltpu.make_async_copy(zero_ref, xs_hbm.at[pl.ds(pl.multiple_of(b * tile, tile), tile)], zsem).start()
            return carry

        def ztail_wait(b, carry):
            pltpu.make_async_copy(zero_ref, xs_hbm.at[pl.ds(0, tile)], zsem).wait()
            return carry

        lax.fori_loop(first_unused, n_blocks, ztail, 0)
        lax.fori_loop(first_unused, n_blocks, ztail_wait, 0)

    def row(t, carry):
        da = gs_ref[e0_ref[0, 0, t]] + r0_ref[0, 0, t]
        db = gs_ref[e1_ref[0, 0, t]] + r1_ref[0, 0, t]
        d0_ref[0, 0, t] = da
        d1_ref[0, 0, t] = db
        pltpu.make_async_copy(u_ref.at[pl.ds(t, 1)], xs_hbm.at[pl.ds(da, 1)], sem).start(priority=0)
        pltpu.make_async_copy(u_ref.at[pl.ds(t, 1)], xs_hbm.at[pl.ds(db, 1)], sem).start(priority=1)
        return carry

    lax.fori_loop(0, td, row, 0, unroll=4)
    for _ in range(2):
        pltpu.make_async_copy(u_ref, xs_hbm.at[pl.ds(0, td)], sem).wait()


def _dispatch(tot_i, e0, e1, r0, r1, u2, n_blocks, tile):
    t, d = u2.shape
    td = min(512, t)
    n_exp = N_EXPERTS
    nt = t // td
    resh = lambda a: a.reshape(nt, 1, td)
    smem_blk = pl.BlockSpec((1, 1, td), lambda i, tot: (i, 0, 0), memory_space=pltpu.SMEM)
    kern = functools.partial(_dispatch_kernel, n_exp=n_exp, n_blocks=n_blocks, tile=tile, td=td)
    grid_spec = pltpu.PrefetchScalarGridSpec(
        num_scalar_prefetch=1,
        grid=(nt,),
        in_specs=[smem_blk, smem_blk, smem_blk, smem_blk,
                  pl.BlockSpec((td, d), lambda i, tot: (i, 0))],
        out_specs=(pl.BlockSpec(memory_space=pl.ANY), smem_blk, smem_blk,
                   pl.BlockSpec((2, n_blocks), lambda i, tot: (0, 0), memory_space=pltpu.SMEM)),
        scratch_shapes=[pltpu.SMEM((n_exp + 1,), I32),
                        pltpu.VMEM((tile, d), F32),
                        pltpu.SemaphoreType.DMA(()),
                        pltpu.SemaphoreType.DMA(())],
    )
    xs, d0, d1, be = pl.pallas_call(
        kern,
        out_shape=(jax.ShapeDtypeStruct((n_blocks * tile, d), F32),
                   jax.ShapeDtypeStruct((nt, 1, td), I32),
                   jax.ShapeDtypeStruct((nt, 1, td), I32),
                   jax.ShapeDtypeStruct((2, n_blocks), I32)),
        grid_spec=grid_spec,
        compiler_params=_cparams(("arbitrary",)),
        name="moe_dispatch",
    )(tot_i, resh(e0), resh(e1), resh(r0), resh(r1), u2)
    return xs, d0.reshape(t), d1.reshape(t), be


def _expert_kernel(be_ref, x_ref, wg_ref, wu_ref, wd_ref, y_ref, xb_ref):
    b = pl.program_id(0)
    f = pl.program_id(1)
    valid = be_ref[1, b] == 1

    @pl.when(f == 0)
    def _():
        y_ref[...] = jnp.zeros_like(y_ref)

    @pl.when(valid)
    def _():
        @pl.when(f == 0)
        def _():
            xb_ref[...] = x_ref[...].astype(BF16)

        xb = xb_ref[...]
        hmid = _silu(_dot(xb, wg_ref[0])) * _dot(xb, wu_ref[0])
        y_ref[...] += _dot(hmid.astype(BF16), wd_ref[0])


def _expert_ffn(be, xs, w_gate, w_up, w_down, tile):
    rows, d = xs.shape
    n_blocks = rows // tile
    fdim = w_gate.shape[2]
    tf = 1024
    nf = fdim // tf

    def fsel(b, f, be_ref):
        return jnp.where(be_ref[1, b] == 1, f, nf - 1)

    grid_spec = pltpu.PrefetchScalarGridSpec(
        num_scalar_prefetch=1,
        grid=(n_blocks, nf),
        in_specs=[pl.BlockSpec((tile, d), lambda b, f, be_ref: (b * be_ref[1, b], 0)),
                  pl.BlockSpec((1, d, tf), lambda b, f, be_ref: (be_ref[0, b], 0, fsel(b, f, be_ref))),
                  pl.BlockSpec((1, d, tf), lambda b, f, be_ref: (be_ref[0, b], 0, fsel(b, f, be_ref))),
                  pl.BlockSpec((1, tf, d), lambda b, f, be_ref: (be_ref[0, b], fsel(b, f, be_ref), 0))],
        out_specs=pl.BlockSpec((tile, d), lambda b, f, be_ref: (b, 0)),
        scratch_shapes=[pltpu.VMEM((tile, d), BF16)],
    )
    return pl.pallas_call(
        _expert_kernel,
        out_shape=jax.ShapeDtypeStruct((rows, d), F32),
        grid_spec=grid_spec,
        compiler_params=_cparams(("parallel", "arbitrary")),
        name="moe_expert_ffn",
    )(be, xs, w_gate, w_up, w_down)


def _combine_kernel(d0_ref, d1_ref, n0_ref, n1_ref, rt_ref, x_ref, gate_ref, g_ref, b_ref, ys_hbm,
                    o_ref, buf_ref, sem, *, alpha, tc, nt):
    i = pl.program_id(0)
    slot = i % 2

    def gather(a_ref, b_ref2, s):
        def start(t, carry):
            pltpu.make_async_copy(ys_hbm.at[pl.ds(a_ref[0, 0, t], 1)],
                                  buf_ref.at[s, 0, pl.ds(t, 1)], sem.at[s]).start(priority=0)
            pltpu.make_async_copy(ys_hbm.at[pl.ds(b_ref2[0, 0, t], 1)],
                                  buf_ref.at[s, 1, pl.ds(t, 1)], sem.at[s]).start(priority=1)
            return carry

        lax.fori_loop(0, tc, start, 0, unroll=4)

    @pl.when(i == 0)
    def _():
        gather(d0_ref, d1_ref, 0)

    @pl.when(i + 1 < nt)
    def _():
        gather(n0_ref, n1_ref, 1 - slot)

    for half in range(2):
        pltpu.make_async_copy(ys_hbm.at[pl.ds(0, tc)], buf_ref.at[slot, half], sem.at[slot]).wait()
    rt = rt_ref[...]
    ffn = rt[:, 2:3] * buf_ref[slot, 0] + rt[:, 3:4] * buf_ref[slot, 1]
    z = alpha * x_ref[...] + gate_ref[0] * ffn
    o_ref[...] = _ln_rows(z) * g_ref[...] + b_ref[...]


def _combine(d0, d1, rt, x2, mod6, mod_row, seq, ln_g, ln_b, ys, alpha):
    t, d = x2.shape
    tc = min(256, seq)
    nt = t // tc
    per = seq // tc
    smem_blk = pl.BlockSpec((1, 1, tc), lambda i: (i, 0, 0), memory_space=pltpu.SMEM)
    next_blk = pl.BlockSpec((1, 1, tc), lambda i: (jnp.minimum(i + 1, nt - 1), 0, 0),
                            memory_space=pltpu.SMEM)
    d0r, d1r = d0.reshape(nt, 1, tc), d1.reshape(nt, 1, tc)
    return pl.pallas_call(
        functools.partial(_combine_kernel, alpha=alpha, tc=tc, nt=nt),
        out_shape=jax.ShapeDtypeStruct((t, d), F32),
        grid=(nt,),
        in_specs=[smem_blk, smem_blk, next_blk, next_blk,
                  pl.BlockSpec((tc, LANES), lambda i: (i, 0)),
                  pl.BlockSpec((tc, d), lambda i: (i, 0)),
                  pl.BlockSpec((1, 1, d), lambda i: (mod_row(i // per) * 6 + 5, 0, 0)),
                  pl.BlockSpec((1, d), lambda i: (0, 0)),
                  pl.BlockSpec((1, d), lambda i: (0, 0)),
                  pl.BlockSpec(memory_space=pl.ANY)],
        out_specs=pl.BlockSpec((tc, d), lambda i: (i, 0)),
        scratch_shapes=[pltpu.VMEM((2, 2, tc, d), F32), pltpu.SemaphoreType.DMA((2,))],
        compiler_params=_cparams(("arbitrary",)),
        name="moe_combine",
    )(d0r, d1r, d0r, d1r, rt, x2, mod6, ln_g.reshape(1, d), ln_b.reshape(1, d), ys)


def _moe_layer(xs, mod6, mod_row, w_router, w_gate, w_up, w_down, ln_g, ln_b, alpha):
    bsz, s, d = xs.shape
    t = bsz * s
    x2 = xs.reshape(t, d)
    tile = min(MOE_ROW_TILE, t)
    n_blocks = (2 * t) // tile + N_EXPERTS
    u2, rt, tot = _router(x2, mod6, mod_row, s, w_router)
    ri = rt[:, :8].astype(I32)
    tot_i = tot[0].astype(I32)
    xs_sorted, d0, d1, be = _dispatch(tot_i, ri[:, 0], ri[:, 1], ri[:, 4], ri[:, 5], u2, n_blocks, tile)
    ys = _expert_ffn(be, xs_sorted, w_gate, w_up, w_down, tile)
    out = _combine(d0, d1, rt, x2, mod6, mod_row, s, ln_g, ln_b, ys, alpha)
    return out.reshape(bsz, s, d)


def kernel(x, c, ctx, c_ctx, w_ada, b_ada, w_in, conv_w, conv_b, mlstm_gate_b, hgrn_lb,
           ret_decay_logit, head_norm_g, w_branch, w_out, post_ln_g, post_ln_b,
           ffn_w_gate, ffn_w_up, ffn_w_down, moe_w_router, moe_w_gate, moe_w_up, moe_w_down):
    bsz, s_lat, d = x.shape
    s_ctx = ctx.shape[1]
    depth = w_ada.shape[0]
    bw = d // 2
    n_gate = 4 * MLSTM_HEADS
    assert bsz < MOD_ROWS - 1 and s_lat % SCAN_CHUNK == 0 and s_ctx % SCAN_CHUNK == 0
    assert s_lat % GRID_W == 0 and (s_lat <= 1024 or s_lat % 1024 == 0)
    alpha = float((2 * depth) ** 0.25)
    ctx_row = MOD_ROWS // 2
    lat_rows = lambda b: b
    ctx_rows = lambda b: b * 0 + ctx_row

    col = {"mqk": 0, "mv": 2 * bw, "mz": 3 * bw, "hq": 4 * bw, "hi": 5 * bw, "hf": 6 * bw,
           "hg": 8 * bw, "rq": 9 * bw, "rk": 9 * bw + bw // 2, "rv": 10 * bw, "rg": 11 * bw,
           "merge": 12 * bw}
    g0 = 4 * bw

    cv = jnp.zeros((MOD_ROWS, d), F32).at[:bsz].set(c).at[ctx_row].set(c_ctx)
    mod = _ada(cv, w_ada, b_ada)

    h = ctx
    for l in range(depth):
        need_ctx = l < depth - 1
        mod6 = mod[l].reshape(MOD_ROWS * 6, 1, d)
        w_l = w_in[l]
        w_main = jnp.concatenate([w_l[:, :g0], w_l[:, g0 + n_gate:]], axis=1).astype(BF16)
        w_gate = jnp.pad(w_l[:, g0:g0 + n_gate], ((0, 0), (0, LANES - n_gate))).astype(BF16)

        y_x, g_x = _inproj(x, mod6, lat_rows, w_main, w_gate)
        y_h, g_h = _inproj(h, mod6, ctx_rows, w_main, w_gate)
        k_scale = float(bw // MLSTM_HEADS) ** -0.5
        qk = (_conv_silu(y_x, conv_w[l], conv_b[l], True, k_scale),
              _conv_silu(y_h, conv_w[l], conv_b[l], False, k_scale))
        y = (y_x, y_h)
        br_m = _mlstm(qk, y, (g_x, g_h), mlstm_gate_b[l], head_norm_g[l, 0], col["mv"], col["mz"])
        br_h = _hgrn(y, hgrn_lb, l, head_norm_g[l, 1], col["hq"], col["hi"], col["hf"], col["hg"], bw)
        br_r = _retention(y, ret_decay_logit[l], head_norm_g[l, 2],
                          col["rq"], col["rk"], col["rv"], col["rg"],
                          bw // RET_HEADS // 2, bw // RET_HEADS)
        w_br = w_branch[l].astype(BF16)
        w_o = w_out[l].astype(BF16)
        mixed = _merge_proj(br_m[0], br_h[0], br_r[0], y_x, w_br, col["merge"])
        x = _down(mixed, w_o, x, mod6, lat_rows, 2, post_ln_g[l, 0], post_ln_b[l, 0], alpha)
        if need_ctx:
            mixed_h = _merge_proj(br_m[1], br_h[1], br_r[1], y_h, w_br, col["merge"])
            h = _down(mixed_h, w_o, h, mod6, ctx_rows, 2, post_ln_g[l, 0], post_ln_b[l, 0], alpha)

        if l % 2 == 0:
            wg = ffn_w_gate[l // 2].astype(BF16)
            wu = ffn_w_up[l // 2].astype(BF16)
            wd = ffn_w_down[l // 2].astype(BF16)
            x = _down(_ffn_up(x, mod6, lat_rows, wg, wu), wd, x, mod6, lat_rows, 5,
                      post_ln_g[l, 1], post_ln_b[l, 1], alpha)
            if need_ctx:
                h = _down(_ffn_up(h, mod6, ctx_rows, wg, wu), wd, h, mod6, ctx_rows, 5,
                          post_ln_g[l, 1], post_ln_b[l, 1], alpha)
        else:
            e = l // 2
            wg = moe_w_gate[e].astype(BF16)
            wu = moe_w_up[e].astype(BF16)
            wd = moe_w_down[e].astype(BF16)
            x = _moe_layer(x, mod6, lat_rows, moe_w_router[e], wg, wu, wd,
                           post_ln_g[l, 1], post_ln_b[l, 1], alpha)
            if need_ctx:
                h = _moe_layer(h, mod6, ctx_rows, moe_w_router[e], wg, wu, wd,
                               post_ln_g[l, 1], post_ln_b[l, 1], alpha)
    return x
```

```python
import functools

import numpy as np
import jax
import jax.numpy as jnp
from jax import lax
from jax.experimental import pallas as pl
from jax.experimental.pallas import tpu as pltpu

F32 = jnp.float32
BF16 = jnp.bfloat16
I32 = jnp.int32

EPS = 1e-6
NEG = -1e30
TINY = 1e-30
LOG2_E = 1.4426950408889634

LANES = 128
SUBLANES = 8
SCAN_CHUNK = 256
VMEM_LIMIT_BYTES = 56 * 1024 * 1024
GRID_W = 64
MLSTM_HEADS = 4
HGRN_HEADS = 8
HGRN_HEADS_PER_STEP = 8
MLSTM_CHUNK = 256
HGRN_CHUNK = 128
RET_HEADS = 4
N_EXPERTS = 8
MOD_ROWS = 16


def _cparams(sem):
    return pltpu.CompilerParams(dimension_semantics=sem, vmem_limit_bytes=VMEM_LIMIT_BYTES)


def _dot(a, b):
    return jnp.dot(a, b, preferred_element_type=F32)


def _dot_nt(a, b):
    return lax.dot_general(a, b, (((1,), (1,)), ((), ())), preferred_element_type=F32)


def _dot_tn(a, b):
    return lax.dot_general(a, b, (((0,), (0,)), ((), ())), preferred_element_type=F32)


def _split_bf16(a):
    hi = a.astype(BF16)
    lo = (a - hi.astype(F32)).astype(BF16)
    return hi, lo


def _silu(a):
    return a * jax.nn.sigmoid(a)


def _ln_rows(a):
    mu = jnp.mean(a, axis=-1, keepdims=True)
    ac = a - mu
    var = jnp.mean(ac * ac, axis=-1, keepdims=True)
    return ac * lax.rsqrt(var + EPS)


def _ada_kernel(cv_ref, w_ref, b_ref, o_ref):
    a = _silu(cv_ref[...])
    a_hi, a_lo = _split_bf16(a)
    w_hi, w_lo = _split_bf16(w_ref[0])
    acc = _dot(a_hi, w_hi) + _dot(a_lo, w_hi) + _dot(a_hi, w_lo)
    o_ref[0] = acc + b_ref[0]


def _ada(cv, w_ada, b_ada):
    n_layers, d, n = w_ada.shape
    tn = 1024
    return pl.pallas_call(
        _ada_kernel,
        out_shape=jax.ShapeDtypeStruct((n_layers, MOD_ROWS, n), F32),
        grid=(n_layers, n // tn),
        in_specs=[pl.BlockSpec((MOD_ROWS, d), lambda l, j: (0, 0)),
                  pl.BlockSpec((1, d, tn), lambda l, j: (l, 0, j)),
                  pl.BlockSpec((1, 1, tn), lambda l, j: (l, 0, j))],
        out_specs=pl.BlockSpec((1, MOD_ROWS, tn), lambda l, j: (l, 0, j)),
        compiler_params=_cparams(("parallel", "parallel")),
        name="ada",
    )(cv, w_ada, b_ada.reshape(n_layers, 1, n))


def _inproj_kernel(x_ref, sh_ref, sc_ref, w_ref, wg_ref, y_ref, g_ref, u_ref):
    j = pl.program_id(2)

    @pl.when(j == 0)
    def _():
        u = _ln_rows(x_ref[0]) * (1.0 + sc_ref[0]) + sh_ref[0]
        ub = u.astype(BF16)
        u_ref[...] = ub
        g_ref[0] = _dot(ub, wg_ref[...])

    y_ref[0] = _dot(u_ref[...], w_ref[...]).astype(y_ref.dtype)


def _inproj(xs, mod6, mod_row, w_main, w_gate):
    bsz, s, d = xs.shape
    n = w_main.shape[1]
    tm = min(1024, s)
    tn = 1024
    row = mod_row
    return pl.pallas_call(
        _inproj_kernel,
        out_shape=(jax.ShapeDtypeStruct((bsz, s, n), BF16),
                   jax.ShapeDtypeStruct((bsz, s, LANES), F32)),
        grid=(bsz, s // tm, n // tn),
        in_specs=[pl.BlockSpec((1, tm, d), lambda b, i, j: (b, i, 0)),
                  pl.BlockSpec((1, 1, d), lambda b, i, j: (row(b) * 6 + 0, 0, 0)),
                  pl.BlockSpec((1, 1, d), lambda b, i, j: (row(b) * 6 + 1, 0, 0)),
                  pl.BlockSpec((d, tn), lambda b, i, j: (0, j)),
                  pl.BlockSpec((d, LANES), lambda b, i, j: (0, 0))],
        out_specs=(pl.BlockSpec((1, tm, tn), lambda b, i, j: (b, i, j)),
                   pl.BlockSpec((1, tm, LANES), lambda b, i, j: (b, i, 0))),
        scratch_shapes=[pltpu.VMEM((tm, d), BF16)],
        compiler_params=_cparams(("parallel", "parallel", "arbitrary")),
        name="inproj",
    )(xs, mod6, mod6, w_main, w_gate)


_CONV_PAD = 72


def _conv_kernel(a_ref, w_ref, b_ref, o_ref, pad_ref, *, seq, on_grid, k_scale, k_from):
    ct = a_ref.shape[2]
    zeros = jnp.zeros((_CONV_PAD, ct), F32)
    pad_ref[pl.ds(0, _CONV_PAD), :] = zeros
    pad_ref[pl.ds(_CONV_PAD + seq, _CONV_PAD), :] = zeros
    pad_ref[pl.ds(_CONV_PAD, seq), :] = a_ref[0].astype(F32)
    w = w_ref[...]
    bias = b_ref[...]
    scale = jnp.where(pl.program_id(1) >= k_from, k_scale, 1.0).astype(F32)

    rc = min(512, seq)
    for r0 in range(0, seq, rc):
        col = lax.broadcasted_iota(I32, (rc, 1), 0) % GRID_W
        acc = jnp.zeros((rc, ct), F32)
        for dr in range(3) if on_grid else (1,):
            for dc in range(3):
                off = (dr - 1) * GRID_W + (dc - 1)
                tap = pad_ref[pl.ds(_CONV_PAD + r0 + off, rc), :]
                if on_grid and dc == 0:
                    tap = jnp.where(col == 0, 0.0, tap)
                elif on_grid and dc == 2:
                    tap = jnp.where(col == GRID_W - 1, 0.0, tap)
                acc = acc + tap * w[dr * 3 + dc:dr * 3 + dc + 1, :]
        o_ref[0, pl.ds(r0, rc), :] = (_silu(acc + bias) * scale).astype(o_ref.dtype)


def _conv_silu(y, conv_w, conv_b, on_grid, k_scale):
    bsz, seq, _ = y.shape
    ch = conv_w.shape[-1]
    ct = 256
    kern = functools.partial(_conv_kernel, seq=seq, on_grid=on_grid, k_scale=k_scale,
                             k_from=(ch // 2) // ct)
    return pl.pallas_call(
        kern,
        out_shape=jax.ShapeDtypeStruct((bsz, seq, ch), BF16),
        grid=(bsz, ch // ct),
        in_specs=[pl.BlockSpec((1, seq, ct), lambda b, j: (b, 0, j)),
                  pl.BlockSpec((9, ct), lambda b, j: (0, j)),
                  pl.BlockSpec((1, ct), lambda b, j: (0, j))],
        out_specs=pl.BlockSpec((1, seq, ct), lambda b, j: (b, 0, j)),
        scratch_shapes=[pltpu.VMEM((seq + 2 * _CONV_PAD, ct), F32)],
        compiler_params=_cparams(("parallel", "parallel")),
        name="conv_silu",
    )(y, conv_w.reshape(9, ch), conv_b.reshape(1, ch))


def _sched(s, n_lat, n_ctx):
    nc = n_lat + n_ctx
    d = s // nc
    p = s - d * nc
    is_ctx = p < n_ctx
    pc = jnp.minimum(p, n_ctx - 1)
    pq = jnp.maximum(p - n_ctx, 0)
    ctx_c = jnp.where(d == 0, pc, n_ctx - 1 - pc)
    lat_c = jnp.where(d == 0, pq, n_lat - 1 - pq)
    return d, p, is_ctx, lat_c, ctx_c


def _scan_specs(n_lat, n_ctx, bp=1, c=SCAN_CHUNK):
    nc = n_lat + n_ctx

    def in_pair(width, col_blk):
        return (pl.BlockSpec((bp, c, width), lambda b, h, s: (b, _sched(s, n_lat, n_ctx)[3], col_blk(h, s))),
                pl.BlockSpec((bp, c, width), lambda b, h, s: (b, _sched(s, n_lat, n_ctx)[4], col_blk(h, s))))

    def out_pair(width, col_blk):
        return (pl.BlockSpec((bp, c, width),
                             lambda b, h, s: (b, _sched(jnp.maximum(s, nc), n_lat, n_ctx)[3], col_blk(h, s))),
                pl.BlockSpec((bp, c, width),
                             lambda b, h, s: (b, _sched(jnp.maximum(s, nc), n_lat, n_ctx)[4], col_blk(h, s))))

    return in_pair, out_pair


def _store_scan_out(is_ctx, val, ol_ref, oc_ref, bi=0):
    @pl.when(is_ctx)
    def _():
        oc_ref[bi] = val.astype(oc_ref.dtype)

    @pl.when(jnp.logical_not(is_ctx))
    def _():
        ol_ref[bi] = val.astype(ol_ref.dtype)


SCAN_BATCH_PER_STEP = 2


def _log_sigmoid(a):
    return jnp.minimum(a, 0.0) - jnp.log1p(jnp.exp(-jnp.abs(a)))


def _prefix_sums(cols, rows, tri_b, nh):
    lane = lax.broadcasted_iota(I32, (1, LANES), 1)
    hi = cols.astype(BF16).astype(F32)
    rhs = jnp.where(lane < nh, hi, pltpu.roll(cols - hi, nh, axis=1)).astype(BF16)
    bc = _dot(tri_b, rhs)
    col = bc + pltpu.roll(bc, LANES - nh, axis=1)
    sub = lax.broadcasted_iota(I32, (SUBLANES, 1), 0)
    rhi = rows.astype(BF16).astype(F32)
    lhs = jnp.where(sub < nh, rhi, pltpu.roll(rows - rhi, nh, axis=0))
    lhs = jnp.concatenate([lhs, jnp.zeros_like(lhs)], axis=0).astype(BF16)
    br = _dot_nt(lhs, tri_b)[0:SUBLANES]
    row = br + pltpu.roll(br, SUBLANES - nh, axis=0)
    return col, row


def _mlstm_kernel(ql_ref, qc_ref, kl_ref, kc_ref, vl_ref, vc_ref, zl_ref, zc_ref, gl_ref, gc_ref,
                  gb_ref, ng_ref, ol_ref, oc_ref, c_ref, n_ref, m_ref, hbuf_ref, *, n_lat, n_ctx):
    d, p, is_ctx, lat_c, ctx_c = _sched(pl.program_id(2), n_lat, n_ctx)

    @pl.when(p == 0)
    def _():
        c_ref[...] = jnp.zeros_like(c_ref)
        n_ref[...] = jnp.zeros_like(n_ref)
        m_ref[...] = jnp.full_like(m_ref, NEG)

    for bi in range(c_ref.shape[0]):
        _mlstm_step(bi, d, is_ctx, lat_c, ctx_c, ql_ref, qc_ref, kl_ref, kc_ref, vl_ref, vc_ref,
                    zl_ref, zc_ref, gl_ref, gc_ref, gb_ref, ng_ref, ol_ref, oc_ref,
                    c_ref.at[bi], n_ref.at[bi], m_ref.at[bi], hbuf_ref.at[bi], n_lat)


def _mlstm_step(bi, d, is_ctx, lat_c, ctx_c, ql_ref, qc_ref, kl_ref, kc_ref, vl_ref, vc_ref,
                zl_ref, zc_ref, gl_ref, gc_ref, gb_ref, ng_ref, ol_ref, oc_ref,
                c_ref, n_ref, m_ref, hbuf_ref, n_lat):
    nh = MLSTM_HEADS
    chunk = jnp.where(is_ctx, n_lat + ctx_c, lat_c)
    pick = lambda a_l, a_c: jnp.where(is_ctx, a_c[bi], a_l[bi])
    c = ql_ref.shape[1]
    dh = c_ref.shape[1]

    gates = pick(gl_ref, gc_ref) + gb_ref[...]
    gates = jnp.where(d == 0, gates, pltpu.roll(gates, LANES - 2 * nh, axis=1))
    lane = lax.broadcasted_iota(I32, (1, LANES), 1)
    r32 = jnp.where(lane < nh, gates, jnp.where(lane < 2 * nh, _log_sigmoid(gates), 0.0))
    rt = r32.T[0:SUBLANES]
    lf_cols = jnp.where(lane < nh, pltpu.roll(r32, LANES - nh, axis=1), 0.0)
    sub = lax.broadcasted_iota(I32, (SUBLANES, 1), 0)
    lf_rows = jnp.where(sub < nh, pltpu.roll(rt, SUBLANES - nh, axis=0), 0.0)

    t_i = lax.broadcasted_iota(I32, (c, c), 0)
    s_i = lax.broadcasted_iota(I32, (c, c), 1)
    tri = (t_i - s_i) * (1 - 2 * d) >= 0
    b_cols, b_rows = _prefix_sums(lf_cols, lf_rows, tri.astype(BF16), nh)
    totals = jnp.sum(lf_cols, axis=0, keepdims=True)

    q_all = pick(ql_ref, qc_ref)
    k_all = pick(kl_ref, kc_ref)
    v_all = pick(vl_ref, vc_ref)
    outs = []
    for j in range(nh):
        li = r32[:, j:j + 1]
        li_row = rt[j:j + 1, :]
        b_col = b_cols[:, j:j + 1]
        b_row = b_rows[j:j + 1, :]
        total = totals[:, j:j + 1]
        q = q_all[:, j * dh:(j + 1) * dh]
        k = k_all[:, j * dh:(j + 1) * dh]
        vb = v_all[:, j * dh:(j + 1) * dh].astype(BF16)

        m_prev = m_ref[j, 0:1, 0:1]
        d_mat = jnp.where(tri, b_col - b_row + li_row, NEG)
        a_inter = b_col + m_prev
        m_t = jnp.maximum(a_inter, jnp.max(d_mat, axis=1, keepdims=True))
        w_inter = jnp.exp(a_inter - m_t)
        s_mat = _dot_nt(q, k) * jnp.exp(d_mat - m_t)
        num = w_inter * _dot(q, c_ref[j].astype(BF16)) + _dot(s_mat.astype(BF16), vb)
        den = (w_inter * jnp.sum(q.astype(F32) * n_ref[j, 0:1, :], axis=1, keepdims=True)
               + jnp.sum(s_mat, axis=1, keepdims=True))
        outs.append(num / jnp.maximum(jnp.abs(den), jnp.exp(-m_t)))

        g_col = total - b_col + li
        m_new = jnp.maximum(total + m_prev, jnp.max(g_col, axis=0, keepdims=True))
        decay = jnp.exp(total + m_prev - m_new)
        kw = k.astype(F32) * jnp.exp(g_col - m_new)
        c_ref[j] = decay * c_ref[j] + _dot_tn(kw.astype(BF16), vb)
        n_new = decay * n_ref[j, 0:1, :] + jnp.sum(kw, axis=0, keepdims=True)
        n_ref[j] = jnp.broadcast_to(n_new, n_ref.shape[1:])
        m_ref[j] = jnp.broadcast_to(m_new, m_ref.shape[1:])

    @pl.when(d == 0)
    def _():
        for j in range(nh):
            hbuf_ref[chunk, :, j * dh:(j + 1) * dh] = outs[j].astype(hbuf_ref.dtype)

    @pl.when(d == 1)
    def _():
        z = pick(zl_ref, zc_ref).astype(F32)
        fin = []
        for j in range(nh):
            sl = slice(j * dh, (j + 1) * dh)
            hn = _ln_rows(hbuf_ref[chunk, :, sl].astype(F32) + outs[j])
            fin.append(hn * ng_ref[:, sl] * _silu(z[:, sl]))
        _store_scan_out(is_ctx, jnp.concatenate(fin, axis=1), ol_ref, oc_ref, bi)


def _mlstm(qk, y, g, gate_b, norm_g, v_col, z_col):
    bsz, s_lat, w2 = qk[0].shape
    s_ctx = qk[1].shape[1]
    width = w2 // 2
    dh = width // MLSTM_HEADS
    c = MLSTM_CHUNK
    n_lat, n_ctx = s_lat // c, s_ctx // c
    bp = 1
    in_pair, out_pair = _scan_specs(n_lat, n_ctx, bp, c)
    kern = functools.partial(_mlstm_kernel, n_lat=n_lat, n_ctx=n_ctx)
    gb = jnp.zeros((1, LANES), F32).at[0, :gate_b.shape[0]].set(gate_b)
    return pl.pallas_call(
        kern,
        out_shape=(jax.ShapeDtypeStruct((bsz, s_lat, width), BF16),
                   jax.ShapeDtypeStruct((bsz, s_ctx, width), BF16)),
        grid=(bsz // bp, 1, 2 * (n_lat + n_ctx)),
        in_specs=[*in_pair(width, lambda h, s: 0),
                  *in_pair(width, lambda h, s: 1),
                  *in_pair(width, lambda h, s: v_col // width),
                  *out_pair(width, lambda h, s: z_col // width),
                  *in_pair(LANES, lambda h, s: 0),
                  pl.BlockSpec((1, LANES), lambda b, h, s: (0, 0)),
                  pl.BlockSpec((1, width), lambda b, h, s: (0, 0))],
        out_specs=out_pair(width, lambda h, s: 0),
        scratch_shapes=[pltpu.VMEM((bp, MLSTM_HEADS, dh, dh), F32),
                        pltpu.VMEM((bp, MLSTM_HEADS, SUBLANES, dh), F32),
                        pltpu.VMEM((bp, MLSTM_HEADS, SUBLANES, LANES), F32),
                        pltpu.VMEM((bp, n_lat + n_ctx, c, width), BF16)],
        compiler_params=_cparams(("parallel", "parallel", "arbitrary")),
        name="mlstm",
    )(qk[0], qk[1], qk[0], qk[1], y[0], y[1], y[0], y[1], g[0], g[1], gb, norm_g.reshape(1, width))


def _ret_kernel(ql_ref, qc_ref, kl_ref, kc_ref, vl_ref, vc_ref, zl_ref, zc_ref, dl_ref, ng_ref,
                ol_ref, oc_ref, s_ref, dmat_ref, hbuf_ref, *, n_lat, n_ctx, q_scale):
    nh = RET_HEADS
    d, p, is_ctx, lat_c, ctx_c = _sched(pl.program_id(2), n_lat, n_ctx)
    chunk = jnp.where(is_ctx, n_lat + ctx_c, lat_c)
    c = SCAN_CHUNK
    bp, dk, dv = s_ref.shape[0], s_ref.shape[2], s_ref.shape[3]

    lane = lax.broadcasted_iota(I32, (1, LANES), 1)
    lgs = _log_sigmoid(dl_ref[...])
    lg = [jnp.sum(jnp.where(lane == d * nh + j, lgs, 0.0), axis=1, keepdims=True) for j in range(nh)]

    @pl.when(p == 0)
    def _():
        s_ref[...] = jnp.zeros_like(s_ref)
        t_i = lax.broadcasted_iota(I32, (c, c), 0)
        s_i = lax.broadcasted_iota(I32, (c, c), 1)
        dist = (t_i - s_i) * (1 - 2 * d)
        distf = jnp.maximum(dist, 0).astype(F32)
        for j in range(nh):
            dmat_ref[j] = jnp.where(dist >= 0, jnp.exp(lg[j] * distf), 0.0)

    t_c = lax.broadcasted_iota(I32, (c, 1), 0)
    pos = jnp.where(d == 0, t_c + 1, c - t_c).astype(F32)
    inter_w = [jnp.exp(lg[j] * pos) for j in range(nh)]
    state_w = [jnp.exp(lg[j] * (c - pos)) for j in range(nh)]
    for bi in range(bp):
        pick = lambda a_l, a_c: jnp.where(is_ctx, a_c[bi], a_l[bi])
        q_all = pick(ql_ref, qc_ref)
        k_all = pick(kl_ref, kc_ref)
        v_all = pick(vl_ref, vc_ref)
        outs = []
        for j in range(nh):
            qb = (q_all[:, j * dk:(j + 1) * dk].astype(F32) * q_scale).astype(BF16)
            kf = k_all[:, j * dk:(j + 1) * dk].astype(F32)
            vb = v_all[:, j * dv:(j + 1) * dv].astype(BF16)
            a = _dot_nt(qb, kf.astype(BF16)) * dmat_ref[j]
            outs.append(inter_w[j] * _dot(qb, s_ref[bi, j].astype(BF16)) + _dot(a.astype(BF16), vb))
            s_ref[bi, j] = (jnp.exp(lg[j] * c) * s_ref[bi, j]
                            + _dot_tn((kf * state_w[j]).astype(BF16), vb))

        @pl.when(d == 0)
        def _():
            for j in range(nh):
                hbuf_ref[bi, chunk, :, j * dv:(j + 1) * dv] = outs[j].astype(hbuf_ref.dtype)

        @pl.when(d == 1)
        def _():
            z = pick(zl_ref, zc_ref).astype(F32)
            fin = []
            for j in range(nh):
                sl = slice(j * dv, (j + 1) * dv)
                hn = _ln_rows(hbuf_ref[bi, chunk, :, sl].astype(F32) + outs[j])
                fin.append(hn * ng_ref[:, sl] * _silu(z[:, sl]))
            _store_scan_out(is_ctx, jnp.concatenate(fin, axis=1), ol_ref, oc_ref, bi)


def _retention(y, decay_logit, norm_g, q_col, k_col, v_col, z_col, dk, dv):
    bsz, s_lat, _ = y[0].shape
    s_ctx = y[1].shape[1]
    c = SCAN_CHUNK
    n_lat, n_ctx = s_lat // c, s_ctx // c
    bp = SCAN_BATCH_PER_STEP if bsz % SCAN_BATCH_PER_STEP == 0 else 1
    in_pair, out_pair = _scan_specs(n_lat, n_ctx, bp)
    kern = functools.partial(_ret_kernel, n_lat=n_lat, n_ctx=n_ctx, q_scale=float(dk) ** -0.5)
    wk, wv = RET_HEADS * dk, RET_HEADS * dv
    dl = jnp.zeros((1, LANES), F32).at[0, :2 * RET_HEADS].set(decay_logit.reshape(-1))
    return pl.pallas_call(
        kern,
        out_shape=(jax.ShapeDtypeStruct((bsz, s_lat, RET_HEADS * dv), BF16),
                   jax.ShapeDtypeStruct((bsz, s_ctx, RET_HEADS * dv), BF16)),
        grid=(bsz // bp, 1, 2 * (n_lat + n_ctx)),
        in_specs=[*in_pair(wk, lambda h, s: q_col // wk),
                  *in_pair(wk, lambda h, s: k_col // wk),
                  *in_pair(wv, lambda h, s: v_col // wv),
                  *out_pair(wv, lambda h, s: z_col // wv),
                  pl.BlockSpec((1, LANES), lambda b, h, s: (0, 0)),
                  pl.BlockSpec((1, wv), lambda b, h, s: (0, 0))],
        out_specs=out_pair(wv, lambda h, s: 0),
        scratch_shapes=[pltpu.VMEM((bp, RET_HEADS, dk, dv), F32),
                        pltpu.VMEM((RET_HEADS, c, c), F32),
                        pltpu.VMEM((bp, n_lat + n_ctx, c, wv), BF16)],
        compiler_params=_cparams(("parallel", "parallel", "arbitrary")),
        name="retention",
    )(y[0], y[1], y[0], y[1], y[0], y[1], y[0], y[1], dl, norm_g.reshape(1, RET_HEADS * dv))


def _hgrn_tables(c):
    nl = int(np.log2(c))
    t = np.arange(c)[:, None]
    u = np.arange(c)[None, :]
    x = (t ^ u).astype(np.int64)
    lvl = np.where(x == 0, nl, np.floor(np.log2(np.maximum(x, 1))).astype(np.int64))
    lvl2 = np.stack([np.where(u <= t, lvl, -1), np.where(u >= t, lvl, -1)])
    return (jnp.asarray((u <= t).astype(np.float32), BF16), jnp.asarray(lvl2, I32), nl)


def _block_ref_rows(b, lev):
    c, w = b.shape
    n = 1 << lev
    if 2 * n >= SUBLANES:
        b3 = b.reshape(c // (2 * n), 2 * n, w)
        return jnp.broadcast_to(b3[:, n - 1:n, :], b3.shape).reshape(c, w)
    t = lax.broadcasted_iota(I32, (c, 1), 0) & (2 * n - 1)
    out = b
    for k in range(2 * n):
        if k != n - 1:
            out = jnp.where(t == k, pltpu.roll(b, (k - (n - 1)) % c, axis=0), out)
    return out


def _hgrn_kernel(ql_ref, qc_ref, vl_ref, vc_ref, fl_ref, fc_ref, zl_ref, zc_ref, lb_ref, ng_ref,
                 tab_ref, lvl_ref, ol_ref, oc_ref, st_ref, hbuf_ref, *, n_lat, n_ctx, layer, nl):
    d, p, is_ctx, lat_c, ctx_c = _sched(pl.program_id(2), n_lat, n_ctx)
    chunk = jnp.where(is_ctx, n_lat + ctx_c, lat_c)
    pick = lambda a_l, a_c: jnp.where(is_ctx, a_c[0], a_l[0])
    c = tab_ref.shape[0]

    @pl.when(p == 0)
    def _():
        st_ref[...] = jnp.zeros_like(st_ref)

    lbp = lb_ref[0]
    sm = jnp.exp(lbp - jnp.max(lbp, axis=0, keepdims=True))
    sm = sm / jnp.sum(sm, axis=0, keepdims=True)
    lb = jnp.zeros((1, lbp.shape[1]), F32)
    for i in range(1, layer + 1):
        lb = lb + sm[i:i + 1, :]

    ft = pick(fl_ref, fc_ref).astype(F32)
    sg = jax.nn.sigmoid(ft)
    kk = (1.0 - lb) * (1.0 - sg)
    lf = jnp.log(jnp.maximum(lb + (1.0 - lb) * sg, TINY)) * LOG2_E
    q = _silu(pick(ql_ref, qc_ref).astype(F32))
    v_all = pick(vl_ref, vc_ref)
    w = lf.shape[1]
    dk = st_ref.shape[1]
    nh = w // dk
    lf_hi = lf.astype(BF16)
    r1 = lf - lf_hi.astype(F32)
    lf_mid = r1.astype(BF16)
    lf_lo = (r1 - lf_mid.astype(F32)).astype(BF16)
    ps = _dot(tab_ref[...], jnp.concatenate([lf_hi, lf_mid, lf_lo], axis=1))
    b = ps[:, :w] + ps[:, w:2 * w] + ps[:, 2 * w:]
    total = jnp.sum(lf, axis=0, keepdims=True)
    bx = b - jnp.where(d == 0, 0.0, 1.0) * lf

    def seg_exp(idx):
        if idx < nl:
            return jnp.exp2(-jnp.abs(bx - _block_ref_rows(b, idx)))
        query_side = (idx == nl)
        from_start = jnp.where(d == 0, 1.0, 0.0) if query_side else jnp.where(d == 0, 0.0, 1.0)
        return jnp.exp2(from_start * bx + (1.0 - from_start) * (total - bx))

    t_c = lax.broadcasted_iota(I32, (c, 1), 0)
    lvl = lvl_ref[0]
    hs = [slice(j * dk, (j + 1) * dk) for j in range(nh)]
    qb = q.astype(BF16)
    kb = kk.astype(BF16)
    accs = [jnp.where(lvl == nl, _dot_nt(qb[:, sl], kb[:, sl]), 0.0) for sl in hs]
    for lev in range(nl):
        is_q = ((t_c >> lev) & 1) != d
        z = (jnp.where(is_q, q, kk) * seg_exp(lev)).astype(BF16)
        hit = lvl == lev
        accs = [jnp.where(hit, _dot_nt(z[:, sl], z[:, sl]), a) for sl, a in zip(hs, accs)]

    qg = (q * seg_exp(nl)).astype(BF16)
    kg = (kk * seg_exp(nl + 1)).astype(BF16)
    decay = jnp.exp2(total)
    outs = []
    for j, sl in enumerate(hs):
        st = st_ref[j]
        vb = v_all[:, sl].astype(BF16)
        outs.append(_dot(accs[j].astype(BF16), vb) + _dot_nt(qg[:, sl], st.astype(BF16)))
        st_ref[j] = st * decay[:, sl] + _dot_tn(vb, kg[:, sl])

    @pl.when(d == 0)
    def _():
        for j, sl in enumerate(hs):
            hbuf_ref[chunk, :, sl] = outs[j]

    @pl.when(d == 1)
    def _():
        z = pick(zl_ref, zc_ref).astype(F32)
        fin = []
        for j, sl in enumerate(hs):
            tot = hbuf_ref[chunk, :, sl] + outs[j]
            hn = tot * lax.rsqrt(jnp.mean(tot * tot, axis=-1, keepdims=True) + EPS)
            fin.append(hn * ng_ref[:, sl] * _silu(z[:, sl]))
        _store_scan_out(is_ctx, jnp.concatenate(fin, axis=1), ol_ref, oc_ref)


def _hgrn(y, hgrn_lb, layer, norm_g, q_col, v_col, f_col, z_col, width):
    bsz, s_lat, _ = y[0].shape
    s_ctx = y[1].shape[1]
    dk = width // HGRN_HEADS
    depth = hgrn_lb.shape[1]
    c = HGRN_CHUNK
    n_lat, n_ctx = s_lat // c, s_ctx // c
    nc = n_lat + n_ctx
    tabs, lvl, nl = _hgrn_tables(c)
    in_pair, out_pair = _scan_specs(n_lat, n_ctx, 1, c)
    kern = functools.partial(_hgrn_kernel, n_lat=n_lat, n_ctx=n_ctx, layer=layer, nl=nl)
    hp = HGRN_HEADS_PER_STEP
    wg = hp * dk
    per_dir = width // wg
    return pl.pallas_call(
        kern,
        out_shape=(jax.ShapeDtypeStruct((bsz, s_lat, width), BF16),
                   jax.ShapeDtypeStruct((bsz, s_ctx, width), BF16)),
        grid=(bsz, HGRN_HEADS // hp, 2 * nc),
        in_specs=[*in_pair(wg, lambda h, s: q_col // wg + h),
                  *in_pair(wg, lambda h, s: v_col // wg + h),
                  *in_pair(wg, lambda h, s: f_col // wg + (s // nc) * per_dir + h),
                  *out_pair(wg, lambda h, s: z_col // wg + h),
                  pl.BlockSpec((1, depth, wg), lambda b, h, s: (s // nc, 0, h)),
                  pl.BlockSpec((1, wg), lambda b, h, s: (0, h)),
                  pl.BlockSpec((c, c), lambda b, h, s: (0, 0)),
                  pl.BlockSpec((1, c, c), lambda b, h, s: (s // nc, 0, 0))],
        out_specs=out_pair(wg, lambda h, s: h),
        scratch_shapes=[pltpu.VMEM((hp, dk, dk), F32),
                        pltpu.VMEM((nc, c, wg), F32)],
        compiler_params=_cparams(("parallel", "parallel", "arbitrary")),
        name="hgrn2",
    )(y[0], y[1], y[0], y[1], y[0], y[1], y[0], y[1], hgrn_lb, norm_g.reshape(1, width), tabs, lvl)


def _merge_kernel(bm_ref, bh_ref, br_ref, g0_ref, g1_ref, g2_ref, w_ref, o_ref):
    acc = jax.nn.sigmoid(g0_ref[0].astype(F32)) * _dot(bm_ref[0], w_ref[0])
    acc = acc + jax.nn.sigmoid(g1_ref[0].astype(F32)) * _dot(bh_ref[0], w_ref[1])
    acc = acc + jax.nn.sigmoid(g2_ref[0].astype(F32)) * _dot(br_ref[0], w_ref[2])
    o_ref[0] = acc.astype(o_ref.dtype)


def _merge_proj(br_m, br_h, br_r, y, w_br, merge_col):
    bsz, s, width = br_m.shape
    d = w_br.shape[2]
    tm = min(1024, s)
    tn = 512
    gcol = merge_col // tn
    nj = d // tn
    bspec = pl.BlockSpec((1, tm, width), lambda b, i, j: (b, i, 0))

    def gspec(n):
        return pl.BlockSpec((1, tm, tn), lambda b, i, j: (b, i, gcol + n * nj + j))

    return pl.pallas_call(
        _merge_kernel,
        out_shape=jax.ShapeDtypeStruct((bsz, s, d), BF16),
        grid=(bsz, s // tm, nj),
        in_specs=[bspec, bspec, bspec, gspec(0), gspec(1), gspec(2),
                  pl.BlockSpec((3, width, tn), lambda b, i, j: (0, 0, j))],
        out_specs=pl.BlockSpec((1, tm, tn), lambda b, i, j: (b, i, j)),
        compiler_params=_cparams(("parallel", "parallel", "arbitrary")),
        name="merge_proj",
    )(br_m, br_h, br_r, y, y, y, w_br)


def _down_kernel(a_ref, w_ref, x_ref, gate_ref, g_ref, b_ref, o_ref, *, alpha, nk):
    k = pl.program_id(2)

    @pl.when(k == 0)
    def _():
        o_ref[...] = jnp.zeros_like(o_ref)

    o_ref[0] += _dot(a_ref[0], w_ref[...])

    @pl.when(k == nk - 1)
    def _():
        z = alpha * x_ref[0] + gate_ref[0] * o_ref[0]
        o_ref[0] = _ln_rows(z) * g_ref[...] + b_ref[...]


def _down(a, w, xs, mod6, mod_row, which, ln_g, ln_b, alpha):
    bsz, s, d = xs.shape
    kdim = w.shape[0]
    tk = next(t for t in (2048, 1024, 512, kdim) if kdim % t == 0)
    nk = kdim // tk
    tm = min(1024, s)
    row = mod_row
    kern = functools.partial(_down_kernel, alpha=alpha, nk=nk)
    return pl.pallas_call(
        kern,
        out_shape=jax.ShapeDtypeStruct((bsz, s, d), F32),
        grid=(bsz, s // tm, nk),
        in_specs=[pl.BlockSpec((1, tm, tk), lambda b, i, k: (b, i, k)),
                  pl.BlockSpec((tk, d), lambda b, i, k: (k, 0)),
                  pl.BlockSpec((1, tm, d), lambda b, i, k: (b, i, 0)),
                  pl.BlockSpec((1, 1, d), lambda b, i, k: (row(b) * 6 + which, 0, 0)),
                  pl.BlockSpec((1, d), lambda b, i, k: (0, 0)),
                  pl.BlockSpec((1, d), lambda b, i, k: (0, 0))],
        out_specs=pl.BlockSpec((1, tm, d), lambda b, i, k: (b, i, 0)),
        compiler_params=_cparams(("parallel", "parallel", "arbitrary")),
        name="down_postnorm",
    )(a, w, xs, mod6, ln_g.reshape(1, d), ln_b.reshape(1, d))


def _ffn_up_kernel(x_ref, sh_ref, sc_ref, wg_ref, wu_ref, h_ref, u_ref):
    @pl.when(pl.program_id(2) == 0)
    def _():
        u = _ln_rows(x_ref[0]) * (1.0 + sc_ref[0]) + sh_ref[0]
        u_ref[...] = u.astype(BF16)

    ub = u_ref[...]
    h_ref[0] = (_silu(_dot(ub, wg_ref[...])) * _dot(ub, wu_ref[...])).astype(h_ref.dtype)


def _ffn_up(xs, mod6, mod_row, w_gate, w_up):
    bsz, s, d = xs.shape
    f = w_gate.shape[1]
    tm = min(1024, s)
    tn = 512
    row = mod_row
    return pl.pallas_call(
        _ffn_up_kernel,
        out_shape=jax.ShapeDtypeStruct((bsz, s, f), BF16),
        grid=(bsz, s // tm, f // tn),
        in_specs=[pl.BlockSpec((1, tm, d), lambda b, i, j: (b, i, 0)),
                  pl.BlockSpec((1, 1, d), lambda b, i, j: (row(b) * 6 + 3, 0, 0)),
                  pl.BlockSpec((1, 1, d), lambda b, i, j: (row(b) * 6 + 4, 0, 0)),
                  pl.BlockSpec((d, tn), lambda b, i, j: (0, j)),
                  pl.BlockSpec((d, tn), lambda b, i, j: (0, j))],
        out_specs=pl.BlockSpec((1, tm, tn), lambda b, i, j: (b, i, j)),
        scratch_shapes=[pltpu.VMEM((tm, d), BF16)],
        compiler_params=_cparams(("parallel", "parallel", "arbitrary")),
        name="ffn_up",
    )(xs, mod6, mod6, w_gate, w_up)


MOE_ROW_TILE = 512


def _router_kernel(x_ref, sh_ref, sc_ref, w_ref, u_ref, rt_ref, tot_ref, carry_ref, *, n_exp):
    i = pl.program_id(0)

    @pl.when(i == 0)
    def _():
        carry_ref[...] = jnp.zeros_like(carry_ref)

    u = _ln_rows(x_ref[...]) * (1.0 + sc_ref[0]) + sh_ref[0]
    u_ref[...] = u
    u_hi, u_lo = _split_bf16(u)
    w_hi, w_lo = _split_bf16(w_ref[...])
    logits = _dot(u_hi, w_hi) + _dot(u_lo, w_hi) + _dot(u_hi, w_lo)
    tr = logits.shape[0]
    lane = lax.broadcasted_iota(I32, (1, LANES), 1)
    ninf = jnp.float32(-jnp.inf)
    lg = jnp.where(lane < n_exp, logits, ninf)
    m1 = jnp.max(lg, axis=1, keepdims=True)
    i1 = jnp.min(jnp.where(lg == m1, lane, LANES), axis=1, keepdims=True)
    lg2 = jnp.where(lane == i1, ninf, lg)
    m2 = jnp.max(lg2, axis=1, keepdims=True)
    i2 = jnp.min(jnp.where(lg2 == m2, lane, LANES), axis=1, keepdims=True)
    e = jnp.exp(m2 - m1)
    w0 = 1.0 / (1.0 + e)
    w1 = e / (1.0 + e)

    oh = ((lane == i1) | (lane == i2)).astype(F32)
    r_i = lax.broadcasted_iota(I32, (tr, tr), 0)
    c_i = lax.broadcasted_iota(I32, (tr, tr), 1)
    cum = _dot((c_i < r_i).astype(BF16), oh.astype(BF16)) + carry_ref[0:1, :]
    r0 = jnp.sum(jnp.where(lane == i1, cum, 0.0), axis=1, keepdims=True)
    r1 = jnp.sum(jnp.where(lane == i2, cum, 0.0), axis=1, keepdims=True)
    new_carry = carry_ref[0:1, :] + jnp.sum(oh, axis=0, keepdims=True)
    carry_ref[...] = jnp.broadcast_to(new_carry, carry_ref.shape)
    tot_ref[...] = jnp.broadcast_to(new_carry, tot_ref.shape)

    rt = jnp.where(lane == 0, i1.astype(F32), 0.0)
    rt = jnp.where(lane == 1, i2.astype(F32), rt)
    rt = jnp.where(lane == 2, w0, rt)
    rt = jnp.where(lane == 3, w1, rt)
    rt = jnp.where(lane == 4, r0, rt)
    rt = jnp.where(lane == 5, r1, rt)
    rt_ref[...] = rt


def _router(x2, mod6, mod_row, seq, w_router):
    t, d = x2.shape
    n_exp = w_router.shape[1]
    tr = min(512, seq)
    wr = jnp.zeros((d, LANES), F32).at[:, :n_exp].set(w_router)
    kern = functools.partial(_router_kernel, n_exp=n_exp)
    per = seq // tr
    return pl.pallas_call(
        kern,
        out_shape=(jax.ShapeDtypeStruct((t, d), F32),
                   jax.ShapeDtypeStruct((t, LANES), F32),
                   jax.ShapeDtypeStruct((SUBLANES, LANES), F32)),
        grid=(t // tr,),
        in_specs=[pl.BlockSpec((tr, d), lambda i: (i, 0)),
                  pl.BlockSpec((1, 1, d), lambda i: (mod_row(i // per) * 6 + 3, 0, 0)),
                  pl.BlockSpec((1, 1, d), lambda i: (mod_row(i // per) * 6 + 4, 0, 0)),
                  pl.BlockSpec((d, LANES), lambda i: (0, 0))],
        out_specs=(pl.BlockSpec((tr, d), lambda i: (i, 0)),
                   pl.BlockSpec((tr, LANES), lambda i: (i, 0)),
                   pl.BlockSpec((SUBLANES, LANES), lambda i: (0, 0))),
        scratch_shapes=[pltpu.VMEM((SUBLANES, LANES), F32)],
        compiler_params=_cparams(("arbitrary",)),
        name="moe_router",
    )(x2, mod6, mod6, wr)


def _dispatch_kernel(tot_ref, e0_ref, e1_ref, r0_ref, r1_ref, u_ref,
                     xs_hbm, d0_ref, d1_ref, be_ref, gs_ref, zero_ref, sem, zsem,
                     *, n_exp, n_blocks, tile, td):
    i = pl.program_id(0)

    @pl.when(i == 0)
    def _():
        start = jnp.int32(0)
        for e in range(n_exp):
            gs_ref[e] = start
            start = start + ((tot_ref[e] + (tile - 1)) // tile) * tile
        gs_ref[n_exp] = start

        def blk(b, carry):
            row = b * tile
            ex = jnp.int32(0)
            for e in range(1, n_exp):
                ex = jnp.where(row >= gs_ref[e], e, ex)
            be_ref[0, b] = ex
            be_ref[1, b] = jnp.where(row < gs_ref[n_exp], 1, 0).astype(I32)
            return carry

        lax.fori_loop(0, n_blocks, blk, 0)

        zero_ref[...] = jnp.zeros_like(zero_ref)
        for e in range(n_exp):
            last = pl.multiple_of(jnp.maximum(gs_ref[e + 1] - tile, 0), tile)
            pltpu.make_async_copy(zero_ref, xs_hbm.at[pl.ds(last, tile)], zsem).start()
        for e in range(n_exp):
            pltpu.make_async_copy(zero_ref, xs_hbm.at[pl.ds(0, tile)], zsem).wait()

        first_unused = gs_ref[n_exp] // tile

        def ztail(b, carry):
            pltpu.make_async_copy(zero_ref, xs_hbm.at[pl.ds(pl.multiple_of(b * tile, tile), tile)], zsem).start()
            return carry

        def ztail_wait(b, carry):
            pltpu.make_async_copy(zero_ref, xs_hbm.at[pl.ds(0, tile)], zsem).wait()
            return carry

        lax.fori_loop(first_unused, n_blocks, ztail, 0)
        lax.fori_loop(first_unused, n_blocks, ztail_wait, 0)

    def row(t, carry):
        da = gs_ref[e0_ref[0, 0, t]] + r0_ref[0, 0, t]
        db = gs_ref[e1_ref[0, 0, t]] + r1_ref[0, 0, t]
        d0_ref[0, 0, t] = da
        d1_ref[0, 0, t] = db
        pltpu.make_async_copy(u_ref.at[pl.ds(t, 1)], xs_hbm.at[pl.ds(da, 1)], sem).start(priority=0)
        pltpu.make_async_copy(u_ref.at[pl.ds(t, 1)], xs_hbm.at[pl.ds(db, 1)], sem).start(priority=1)
        return carry

    lax.fori_loop(0, td, row, 0, unroll=4)
    for _ in range(2):
        pltpu.make_async_copy(u_ref, xs_hbm.at[pl.ds(0, td)], sem).wait()


def _dispatch(tot_i, e0, e1, r0, r1, u2, n_blocks, tile):
    t, d = u2.shape
    td = min(512, t)
    n_exp = N_EXPERTS
    nt = t // td
    resh = lambda a: a.reshape(nt, 1, td)
    smem_blk = pl.BlockSpec((1, 1, td), lambda i, tot: (i, 0, 0), memory_space=pltpu.SMEM)
    kern = functools.partial(_dispatch_kernel, n_exp=n_exp, n_blocks=n_blocks, tile=tile, td=td)
    grid_spec = pltpu.PrefetchScalarGridSpec(
        num_scalar_prefetch=1,
        grid=(nt,),
        in_specs=[smem_blk, smem_blk, smem_blk, smem_blk,
                  pl.BlockSpec((td, d), lambda i, tot: (i, 0))],
        out_specs=(pl.BlockSpec(memory_space=pl.ANY), smem_blk, smem_blk,
                   pl.BlockSpec((2, n_blocks), lambda i, tot: (0, 0), memory_space=pltpu.SMEM)),
        scratch_shapes=[pltpu.SMEM((n_exp + 1,), I32),
                        pltpu.VMEM((tile, d), F32),
                        pltpu.SemaphoreType.DMA(()),
                        pltpu.SemaphoreType.DMA(())],
    )
    xs, d0, d1, be = pl.pallas_call(
        kern,
        out_shape=(jax.ShapeDtypeStruct((n_blocks * tile, d), F32),
                   jax.ShapeDtypeStruct((nt, 1, td), I32),
                   jax.ShapeDtypeStruct((nt, 1, td), I32),
                   jax.ShapeDtypeStruct((2, n_blocks), I32)),
        grid_spec=grid_spec,
        compiler_params=_cparams(("arbitrary",)),
        name="moe_dispatch",
    )(tot_i, resh(e0), resh(e1), resh(r0), resh(r1), u2)
    return xs, d0.reshape(t), d1.reshape(t), be


def _expert_kernel(be_ref, x_ref, wg_ref, wu_ref, wd_ref, y_ref, xb_ref):
    b = pl.program_id(0)
    f = pl.program_id(1)
    valid = be_ref[1, b] == 1

    @pl.when(f == 0)
    def _():
        y_ref[...] = jnp.zeros_like(y_ref)

    @pl.when(valid)
    def _():
        @pl.when(f == 0)
        def _():
            xb_ref[...] = x_ref[...].astype(BF16)

        xb = xb_ref[...]
        hmid = _silu(_dot(xb, wg_ref[0])) * _dot(xb, wu_ref[0])
        y_ref[...] += _dot(hmid.astype(BF16), wd_ref[0])


def _expert_ffn(be, xs, w_gate, w_up, w_down, tile):
    rows, d = xs.shape
    n_blocks = rows // tile
    fdim = w_gate.shape[2]
    tf = 1024
    nf = fdim // tf

    def fsel(b, f, be_ref):
        return jnp.where(be_ref[1, b] == 1, f, nf - 1)

    grid_spec = pltpu.PrefetchScalarGridSpec(
        num_scalar_prefetch=1,
        grid=(n_blocks, nf),
        in_specs=[pl.BlockSpec((tile, d), lambda b, f, be_ref: (b * be_ref[1, b], 0)),
                  pl.BlockSpec((1, d, tf), lambda b, f, be_ref: (be_ref[0, b], 0, fsel(b, f, be_ref))),
                  pl.BlockSpec((1, d, tf), lambda b, f, be_ref: (be_ref[0, b], 0, fsel(b, f, be_ref))),
                  pl.BlockSpec((1, tf, d), lambda b, f, be_ref: (be_ref[0, b], fsel(b, f, be_ref), 0))],
        out_specs=pl.BlockSpec((tile, d), lambda b, f, be_ref: (b, 0)),
        scratch_shapes=[pltpu.VMEM((tile, d), BF16)],
    )
    return pl.pallas_call(
        _expert_kernel,
        out_shape=jax.ShapeDtypeStruct((rows, d), F32),
        grid_spec=grid_spec,
        compiler_params=_cparams(("parallel", "arbitrary")),
        name="moe_expert_ffn",
    )(be, xs, w_gate, w_up, w_down)


def _combine_kernel(d0_ref, d1_ref, n0_ref, n1_ref, rt_ref, x_ref, gate_ref, g_ref, b_ref, ys_hbm,
                    o_ref, buf_ref, sem, *, alpha, tc, nt):
    i = pl.program_id(0)
    slot = i % 2

    def gather(a_ref, b_ref2, s):
        def start(t, carry):
            pltpu.make_async_copy(ys_hbm.at[pl.ds(a_ref[0, 0, t], 1)],
                                  buf_ref.at[s, 0, pl.ds(t, 1)], sem.at[s]).start(priority=0)
            pltpu.make_async_copy(ys_hbm.at[pl.ds(b_ref2[0, 0, t], 1)],
                                  buf_ref.at[s, 1, pl.ds(t, 1)], sem.at[s]).start(priority=1)
            return carry

        lax.fori_loop(0, tc, start, 0, unroll=4)

    @pl.when(i == 0)
    def _():
        gather(d0_ref, d1_ref, 0)

    @pl.when(i + 1 < nt)
    def _():
        gather(n0_ref, n1_ref, 1 - slot)

    for half in range(2):
        pltpu.make_async_copy(ys_hbm.at[pl.ds(0, tc)], buf_ref.at[slot, half], sem.at[slot]).wait()
    rt = rt_ref[...]
    ffn = rt[:, 2:3] * buf_ref[slot, 0] + rt[:, 3:4] * buf_ref[slot, 1]
    z = alpha * x_ref[...] + gate_ref[0] * ffn
    o_ref[...] = _ln_rows(z) * g_ref[...] + b_ref[...]


def _combine(d0, d1, rt, x2, mod6, mod_row, seq, ln_g, ln_b, ys, alpha):
    t, d = x2.shape
    tc = min(256, seq)
    nt = t // tc
    per = seq // tc
    smem_blk = pl.BlockSpec((1, 1, tc), lambda i: (i, 0, 0), memory_space=pltpu.SMEM)
    next_blk = pl.BlockSpec((1, 1, tc), lambda i: (jnp.minimum(i + 1, nt - 1), 0, 0),
                            memory_space=pltpu.SMEM)
    d0r, d1r = d0.reshape(nt, 1, tc), d1.reshape(nt, 1, tc)
    return pl.pallas_call(
        functools.partial(_combine_kernel, alpha=alpha, tc=tc, nt=nt),
        out_shape=jax.ShapeDtypeStruct((t, d), F32),
        grid=(nt,),
        in_specs=[smem_blk, smem_blk, next_blk, next_blk,
                  pl.BlockSpec((tc, LANES), lambda i: (i, 0)),
                  pl.BlockSpec((tc, d), lambda i: (i, 0)),
                  pl.BlockSpec((1, 1, d), lambda i: (mod_row(i // per) * 6 + 5, 0, 0)),
                  pl.BlockSpec((1, d), lambda i: (0, 0)),
                  pl.BlockSpec((1, d), lambda i: (0, 0)),
                  pl.BlockSpec(memory_space=pl.ANY)],
        out_specs=pl.BlockSpec((tc, d), lambda i: (i, 0)),
        scratch_shapes=[pltpu.VMEM((2, 2, tc, d), F32), pltpu.SemaphoreType.DMA((2,))],
        compiler_params=_cparams(("arbitrary",)),
        name="moe_combine",
    )(d0r, d1r, d0r, d1r, rt, x2, mod6, ln_g.reshape(1, d), ln_b.reshape(1, d), ys)


def _moe_layer(xs, mod6, mod_row, w_router, w_gate, w_up, w_down, ln_g, ln_b, alpha):
    bsz, s, d = xs.shape
    t = bsz * s
    x2 = xs.reshape(t, d)
    tile = min(MOE_ROW_TILE, t)
    n_blocks = (2 * t) // tile + N_EXPERTS
    u2, rt, tot = _router(x2, mod6, mod_row, s, w_router)
    ri = rt[:, :8].astype(I32)
    tot_i = tot[0].astype(I32)
    xs_sorted, d0, d1, be = _dispatch(tot_i, ri[:, 0], ri[:, 1], ri[:, 4], ri[:, 5], u2, n_blocks, tile)
    ys = _expert_ffn(be, xs_sorted, w_gate, w_up, w_down, tile)
    out = _combine(d0, d1, rt, x2, mod6, mod_row, s, ln_g, ln_b, ys, alpha)
    return out.reshape(bsz, s, d)


def kernel(x, c, ctx, c_ctx, w_ada, b_ada, w_in, conv_w, conv_b, mlstm_gate_b, hgrn_lb,
           ret_decay_logit, head_norm_g, w_branch, w_out, post_ln_g, post_ln_b,
           ffn_w_gate, ffn_w_up, ffn_w_down, moe_w_router, moe_w_gate, moe_w_up, moe_w_down):
    bsz, s_lat, d = x.shape
    s_ctx = ctx.shape[1]
    depth = w_ada.shape[0]
    bw = d // 2
    n_gate = 4 * MLSTM_HEADS
    assert bsz < MOD_ROWS - 1 and s_lat % SCAN_CHUNK == 0 and s_ctx % SCAN_CHUNK == 0
    assert s_lat % GRID_W == 0 and (s_lat <= 1024 or s_lat % 1024 == 0)
    assert bsz * s_ctx <= 1024 or (bsz * s_ctx) % 1024 == 0
    alpha = float((2 * depth) ** 0.25)
    ctx_row = MOD_ROWS // 2
    lat_rows = lambda b: b
    ctx_rows = lambda b: b * 0 + ctx_row

    col = {"mqk": 0, "mv": 2 * bw, "mz": 3 * bw, "hq": 4 * bw, "hi": 5 * bw, "hf": 6 * bw,
           "hg": 8 * bw, "rq": 9 * bw, "rk": 9 * bw + bw // 2, "rv": 10 * bw, "rg": 11 * bw,
           "merge": 12 * bw}
    g0 = 4 * bw

    cv = jnp.zeros((MOD_ROWS, d), F32).at[:bsz].set(c).at[ctx_row].set(c_ctx)
    mod = _ada(cv, w_ada, b_ada)

    h = ctx
    for l in range(depth):
        need_ctx = l < depth - 1
        mod6 = mod[l].reshape(MOD_ROWS * 6, 1, d)
        w_l = w_in[l]
        w_main = jnp.concatenate([w_l[:, :g0], w_l[:, g0 + n_gate:]], axis=1).astype(BF16)
        w_gate = jnp.pad(w_l[:, g0:g0 + n_gate], ((0, 0), (0, LANES - n_gate))).astype(BF16)

        y_x, g_x = _inproj(x, mod6, lat_rows, w_main, w_gate)
        flat = lambda a: a.reshape(1, bsz * s_ctx, a.shape[-1])
        unflat = lambda a: a.reshape(bsz, s_ctx, a.shape[-1])
        y_h, g_h = (unflat(a) for a in _inproj(flat(h), mod6, ctx_rows, w_main, w_gate))
        k_scale = float(bw // MLSTM_HEADS) ** -0.5
        qk = (_conv_silu(y_x, conv_w[l], conv_b[l], True, k_scale),
              _conv_silu(y_h, conv_w[l], conv_b[l], False, k_scale))
        y = (y_x, y_h)
        br_m = _mlstm(qk, y, (g_x, g_h), mlstm_gate_b[l], head_norm_g[l, 0], col["mv"], col["mz"])
        br_h = _hgrn(y, hgrn_lb, l, head_norm_g[l, 1], col["hq"], col["hi"], col["hf"], col["hg"], bw)
        br_r = _retention(y, ret_decay_logit[l], head_norm_g[l, 2],
                          col["rq"], col["rk"], col["rv"], col["rg"],
                          bw // RET_HEADS // 2, bw // RET_HEADS)
        w_br = w_branch[l].astype(BF16)
        w_o = w_out[l].astype(BF16)
        mixed = _merge_proj(br_m[0], br_h[0], br_r[0], y_x, w_br, col["merge"])
        x = _down(mixed, w_o, x, mod6, lat_rows, 2, post_ln_g[l, 0], post_ln_b[l, 0], alpha)
        if need_ctx:
            mixed_h = _merge_proj(flat(br_m[1]), flat(br_h[1]), flat(br_r[1]), flat(y_h), w_br,
                                  col["merge"])
            h = unflat(_down(mixed_h, w_o, flat(h), mod6, ctx_rows, 2,
                             post_ln_g[l, 0], post_ln_b[l, 0], alpha))

        if l % 2 == 0:
            wg = ffn_w_gate[l // 2].astype(BF16)
            wu = ffn_w_up[l // 2].astype(BF16)
            wd = ffn_w_down[l // 2].astype(BF16)
            x = _down(_ffn_up(x, mod6, lat_rows, wg, wu), wd, x, mod6, lat_rows, 5,
                      post_ln_g[l, 1], post_ln_b[l, 1], alpha)
            if need_ctx:
                h = unflat(_down(_ffn_up(flat(h), mod6, ctx_rows, wg, wu), wd, flat(h), mod6, ctx_rows, 5,
                                 post_ln_g[l, 1], post_ln_b[l, 1], alpha))
        else:
            e = l // 2
            wg = moe_w_gate[e].astype(BF16)
            wu = moe_w_up[e].astype(BF16)
            wd = moe_w_down[e].astype(BF16)
            x = _moe_layer(x, mod6, lat_rows, moe_w_router[e], wg, wu, wd,
                           post_ln_g[l, 1], post_ln_b[l, 1], alpha)
            if need_ctx:
                h = _moe_layer(h, mod6, ctx_rows, moe_w_router[e], wg, wu, wd,
                               post_ln_g[l, 1], post_ln_b[l, 1], alpha)
    return x
```
